```python
import math
import jax, jax.numpy as jnp
from jax import lax
import numpy as np

D_MODEL = 1024
BATCH = 8
SEQ = 2048
DEPTH = 2
DEC_BATCH = 128
DEC_SEQ = 1
PAST_LEN = 16384
PAGE_SIZE = 128

F32 = jnp.float32
MIX_WIDTH = D_MODEL // 2
N_BRANCH = 4
NORM_EPS = 1e-6
CHUNK = 64
S5_GROUP = 16
S5_GROUPS = MIX_WIDTH // S5_GROUP
S5_STATE = 64
S5_DT_MIN = 0.001
S5_DT_MAX = 0.1
RET_HEADS = 4
RET_DK = 64
RET_DV = MIX_WIDTH // RET_HEADS
ROPE_BASE = 10000.0
LRU_WIDTH = MIX_WIDTH
LRU_BLOCKS = 8
LRU_BLOCK = LRU_WIDTH // LRU_BLOCKS
CONV_WIDTH = 4
LRU_C = 8.0
GLA_HEADS = 4
GLA_DK = 64
GLA_DV = MIX_WIDTH // GLA_HEADS
GLA_GATE_RANK = 16
GLA_GATE_TAU = 16.0
PEER_HEADS = 8
PEER_NKEYS = 128
PEER_EXPERTS = PEER_NKEYS * PEER_NKEYS
PEER_DQ = 256
PEER_TOPK = 16
PEER_BLOCK = 128

IN_SIZES = (MIX_WIDTH,
            RET_HEADS * RET_DK, RET_HEADS * RET_DK, RET_HEADS * RET_DV, MIX_WIDTH,
            LRU_WIDTH, LRU_WIDTH,
            GLA_HEADS * GLA_DK, GLA_HEADS * GLA_DK, GLA_HEADS * GLA_DV, GLA_GATE_RANK, MIX_WIDTH,
            N_BRANCH * D_MODEL)
IN_WIDTH = sum(IN_SIZES)
IN_CUTS = tuple(sum(IN_SIZES[:i + 1]) for i in range(len(IN_SIZES) - 1))

kernel_name = 'hybrid_s5_retnet_rglru_gla_peer_step'


def _rmsnorm(x, g):
    xf = x.astype(F32)
    y = xf * lax.rsqrt(jnp.mean(xf * xf, axis=-1, keepdims=True) + NORM_EPS)
    return (y * g).astype(x.dtype)


def _head_norm(o, g):
    b, l, h, e = o.shape
    mu = jnp.mean(o, axis=-1, keepdims=True)
    var = jnp.mean(jnp.square(o - mu), axis=-1, keepdims=True)
    return ((o - mu) * lax.rsqrt(var + NORM_EPS)).reshape(b, l, h * e) * g


def _chunk_len(L):
    return CHUNK if L % CHUNK == 0 else L


def _rotary(x, pos):
    half = x.shape[-1] // 2
    inv = ROPE_BASE ** (-jnp.arange(half, dtype=F32) / half)
    ang = pos.astype(F32)[:, None] * inv[None, :]
    cos = jnp.cos(ang)[None, :, None, :]
    sin = jnp.sin(ang)[None, :, None, :]
    x1, x2 = x[..., :half], x[..., half:]
    return jnp.concatenate([x1 * cos - x2 * sin, x1 * sin + x2 * cos], axis=-1)


def _s5(u, h_re, h_im, lam_re, lam_im, log_dt, b_re, b_im, c_re, c_im, d_skip, w_glu, b_glu):
    bsz, L, _ = u.shape
    ug = u.astype(F32).reshape(bsz, L, S5_GROUPS, S5_GROUP)
    dt = jnp.exp(log_dt.astype(F32))[:, None]
    lr = lam_re.astype(F32)
    li = lam_im.astype(F32)
    mag = jnp.exp(lr * dt)
    ab_re = mag * jnp.cos(li * dt)
    ab_im = mag * jnp.sin(li * dt)
    den = lr * lr + li * li
    zr = ((ab_re - 1.0) * lr + ab_im * li) / den
    zi = (ab_im * lr - (ab_re - 1.0) * li) / den
    bb_re = zr[..., None] * b_re - zi[..., None] * b_im
    bb_im = zr[..., None] * b_im + zi[..., None] * b_re
    w_re = jnp.einsum('blgc,gpc->blgp', ug, bb_re)
    w_im = jnp.einsum('blgc,gpc->blgp', ug, bb_im)
    w_re = w_re.at[:, 0].add(ab_re * h_re - ab_im * h_im)
    w_im = w_im.at[:, 0].add(ab_re * h_im + ab_im * h_re)
    a_re = jnp.broadcast_to(ab_re, w_re.shape)
    a_im = jnp.broadcast_to(ab_im, w_im.shape)

    def combine(e1, e2):
        a1r, a1i, b1r, b1i = e1
        a2r, a2i, b2r, b2i = e2
        return (a2r * a1r - a2i * a1i, a2r * a1i + a2i * a1r,
                a2r * b1r - a2i * b1i + b2r, a2r * b1i + a2i * b1r + b2i)

    _, _, s_re, s_im = lax.associative_scan(combine, (a_re, a_im, w_re, w_im), axis=1)
    y = (jnp.einsum('blgp,gcp->blgc', s_re, c_re) - jnp.einsum('blgp,gcp->blgc', s_im, c_im)
         + d_skip.reshape(S5_GROUPS, S5_GROUP) * ug).reshape(bsz, L, MIX_WIDTH)
    y = jax.nn.gelu(y)
    y = y * jax.nn.sigmoid(y @ w_glu + b_glu)
    return y, s_re[:, -1], s_im[:, -1]


def _retention(q, k, v, s0):
    bsz, L, H, dk = q.shape
    dv = v.shape[-1]
    C = _chunk_len(L)
    n = L // C
    log_g = jnp.log1p(-jnp.exp2(-5.0 - jnp.arange(H, dtype=F32)))
    qc = q.reshape(bsz, n, C, H, dk)
    kc = k.reshape(bsz, n, C, H, dk)
    vc = v.reshape(bsz, n, C, H, dv)
    idx = jnp.arange(C, dtype=F32)
    diff = idx[:, None] - idx[None, :]
    mask = diff >= 0
    dec = jnp.where(mask[None], jnp.exp(log_g[:, None, None] * jnp.where(mask, diff, 0.0)[None]), 0.0)
    scores = jnp.einsum('bnthd,bnshd->bnhts', qc, kc) * dec
    o_intra = jnp.einsum('bnhts,bnshe->bnthe', scores, vc)
    w_state = jnp.exp(log_g[:, None] * (C - 1.0 - idx)[None, :])
    kv = jnp.einsum('bnshd,hs,bnshe->bnhde', kc, w_state, vc)
    g_chunk = jnp.exp(log_g * C)[None, :, None, None]

    def step(s, kv_n):
        return g_chunk * s + kv_n, s

    s_fin, s_prev = lax.scan(step, s0, jnp.moveaxis(kv, 1, 0))
    w_q = jnp.exp(log_g[:, None] * (idx + 1.0)[None, :])
    o_inter = jnp.einsum('bnthd,ht,nbhde->bnthe', qc, w_q, s_prev)
    return (o_intra + o_inter).reshape(bsz, L, H, dv), s_fin


def _gla(q, k, v, log_a, s0):
    bsz, L, H, dk = q.shape
    dv = v.shape[-1]
    C = _chunk_len(L)
    n = L // C
    qc = q.reshape(bsz, n, C, H, dk)
    kc = k.reshape(bsz, n, C, H, dk)
    vc = v.reshape(bsz, n, C, H, dv)
    b = jnp.cumsum(log_a.reshape(bsz, n, C, H, dk), axis=2)
    b_last = b[:, :, -1]
    q_t = qc * jnp.exp(b)
    k_t = kc * jnp.exp(-b)
    k_s = kc * jnp.exp(b_last[:, :, None] - b)
    mask = jnp.tril(jnp.ones((C, C), dtype=bool))
    scores = jnp.where(mask, jnp.einsum('bnthd,bnshd->bnhts', q_t, k_t), 0.0)
    o_intra = jnp.einsum('bnhts,bnshe->bnthe', scores, vc)
    kv = jnp.einsum('bnshd,bnshe->bnhde', k_s, vc)
    decay = jnp.exp(b_last)

    def step(s, xs):
        kv_n, d_n = xs
        return d_n[..., None] * s + kv_n, s

    s_fin, s_prev = lax.scan(step, s0, (jnp.moveaxis(kv, 1, 0), jnp.moveaxis(decay, 1, 0)))
    o_inter = jnp.einsum('bnthd,nbhde->bnthe', q_t, s_prev)
    return (o_intra + o_inter).reshape(bsz, L, H, dv), s_fin


def _rglru(xb, gb, h0, conv_buf, conv_w, conv_b, w_a, b_a, w_x, b_x, lam):
    bsz, L, W = xb.shape
    xp = jnp.concatenate([conv_buf.astype(F32), xb.astype(F32)], axis=1)
    xc = conv_b + sum(xp[:, j:j + L] * conv_w[j] for j in range(CONV_WIDTH))
    new_buf = xp[:, -(CONV_WIDTH - 1):]
    xr = xc.reshape(bsz, L, LRU_BLOCKS, LRU_BLOCK)
    r = jax.nn.sigmoid(jnp.einsum('blhi,hij->blhj', xr, w_a).reshape(bsz, L, W) + b_a)
    i = jax.nn.sigmoid(jnp.einsum('blhi,hij->blhj', xr, w_x).reshape(bsz, L, W) + b_x)
    log_a = -LRU_C * r * jax.nn.softplus(-lam)
    a = jnp.exp(log_a)
    bterm = jnp.sqrt(-jnp.expm1(2.0 * log_a)) * (i * xc)
    bterm = bterm.at[:, 0].add(a[:, 0] * h0)

    def combine(e1, e2):
        a1, b1 = e1
        a2, b2 = e2
        return a2 * a1, a2 * b1 + b2

    _, h = lax.associative_scan(combine, (a, bterm), axis=1)
    y = h * jax.nn.gelu(gb.astype(F32))
    return y, h[:, -1], new_buf


def _peer(x, w_q, sub_keys, u_tab, v_tab):
    bsz, L, D = x.shape
    T = bsz * L
    n_blk = -(-T // PEER_BLOCK)
    xt = jnp.pad(x.reshape(T, D), ((0, n_blk * PEER_BLOCK - T), (0, 0)))

    def block(xb):
        q = (xb @ w_q).reshape(PEER_BLOCK, PEER_HEADS, 2, PEER_DQ // 2)
        s = jnp.einsum('thpc,hpkc->thpk', q, sub_keys).astype(F32)
        sv, si = lax.top_k(s, PEER_TOPK)
        cand = (sv[:, :, 0, :, None] + sv[:, :, 1, None, :]).reshape(PEER_BLOCK, PEER_HEADS, PEER_TOPK * PEER_TOPK)
        cand_idx = (si[:, :, 0, :, None] * PEER_NKEYS + si[:, :, 1, None, :]).reshape(PEER_BLOCK, PEER_HEADS, PEER_TOPK * PEER_TOPK)
        fv, fi = lax.top_k(cand, PEER_TOPK)
        eidx = jnp.take_along_axis(cand_idx, fi, axis=-1)
        g = jax.nn.softmax(fv, axis=-1)
        hid = jax.nn.gelu(jnp.einsum('thkd,td->thk', u_tab[eidx], xb))
        return jnp.einsum('thk,thkd->td', g * hid, v_tab[eidx])

    out = lax.map(block, xt.reshape(n_blk, PEER_BLOCK, D))
    return out.reshape(n_blk * PEER_BLOCK, D)[:T].reshape(bsz, L, D)


def _zero_state(b):
    return (jnp.zeros((b, S5_GROUPS, S5_STATE), F32), jnp.zeros((b, S5_GROUPS, S5_STATE), F32),
            jnp.zeros((b, RET_HEADS, RET_DK, RET_DV), F32), jnp.zeros((b, LRU_WIDTH), F32),
            jnp.zeros((b, CONV_WIDTH - 1, LRU_WIDTH), F32), jnp.zeros((b, GLA_HEADS, GLA_DK, GLA_DV), F32))


def _layer(x, pos, st, lp):
    s5_re0, s5_im0, ret0, lru0, conv0, gla0 = st
    bsz, L, _ = x.shape
    n = _rmsnorm(x, lp['norm_mix'])
    (u_s5, rq, rk, rv, rg, lx, lg, gq, gk, gv, ga, gg, mg) = jnp.split(n @ lp['w_in'], IN_CUTS, axis=-1)
    y_a, s5_re, s5_im = _s5(u_s5, s5_re0, s5_im0, lp['s5_lambda_re'], lp['s5_lambda_im'], lp['s5_log_dt'],
                            lp['s5_b_re'], lp['s5_b_im'], lp['s5_c_re'], lp['s5_c_im'], lp['s5_d'],
                            lp['s5_w_glu'], lp['s5_b_glu'])
    q = _rotary(rq.astype(F32).reshape(bsz, L, RET_HEADS, RET_DK), pos)
    k = _rotary(rk.astype(F32).reshape(bsz, L, RET_HEADS, RET_DK), pos) * (RET_DK ** -0.5)
    o, ret = _retention(q, k, rv.astype(F32).reshape(bsz, L, RET_HEADS, RET_DV), ret0)
    y_b = jax.nn.silu(rg.astype(F32)) * _head_norm(o, lp['ret_norm'])
    y_c, lru, conv = _rglru(lx, lg, lru0, conv0, lp['lru_conv_w'], lp['lru_conv_b'], lp['lru_w_a'],
                            lp['lru_b_a'], lp['lru_w_x'], lp['lru_b_x'], lp['lru_lambda'])
    log_a = jax.nn.log_sigmoid((ga @ lp['gla_w_a2'] + lp['gla_b_a2']).astype(F32)) / GLA_GATE_TAU
    o, gla = _gla(gq.astype(F32).reshape(bsz, L, GLA_HEADS, GLA_DK) * (GLA_DK ** -0.5),
                  gk.astype(F32).reshape(bsz, L, GLA_HEADS, GLA_DK),
                  gv.astype(F32).reshape(bsz, L, GLA_HEADS, GLA_DV),
                  log_a.reshape(bsz, L, GLA_HEADS, GLA_DK), gla0)
    y_d = jax.nn.silu(gg.astype(F32)) * _head_norm(o, lp['gla_norm'])
    ys = jnp.stack([y_a, y_b, y_c, y_d], axis=2)
    z = jnp.einsum('blnm,nmd->blnd', ys, lp['w_branch'])
    gates = jax.nn.sigmoid(mg.astype(F32).reshape(bsz, L, N_BRANCH, D_MODEL))
    h = x + (jnp.sum(gates * z, axis=2) @ lp['w_out']).astype(x.dtype)
    h = h + _peer(_rmsnorm(h, lp['norm_ffn']), lp['peer_w_q'], lp['peer_sub_keys'],
                  lp['peer_u'], lp['peer_v']).astype(h.dtype)
    return h, (s5_re, s5_im, ret, lru, conv, gla)


def setup_inputs(seed: int = 0) -> dict:
    key = jax.random.key(seed)
    ks = iter(jax.random.split(key, 64))

    def nrm(shape, scale):
        return jax.random.normal(next(ks), shape, F32) * scale

    def gain(shape):
        return 1.0 + nrm(shape, 0.02)

    L = DEPTH
    n_idx = jnp.arange(S5_STATE, dtype=F32)
    u_lru = jax.random.uniform(next(ks), (L, LRU_WIDTH), F32, 0.9, 0.999)
    a0 = u_lru ** (1.0 / LRU_C)
    return {
        'x_prompt': nrm((BATCH, SEQ, D_MODEL), 1.0),
        'x_sample': nrm((DEC_BATCH, DEC_SEQ, D_MODEL), 1.0),
        'state_s5_re': nrm((L, DEC_BATCH, S5_GROUPS, S5_STATE), 0.1),
        'state_s5_im': nrm((L, DEC_BATCH, S5_GROUPS, S5_STATE), 0.1),
        'state_ret': nrm((L, DEC_BATCH, RET_HEADS, RET_DK, RET_DV), 1.0),
        'state_lru': nrm((L, DEC_BATCH, LRU_WIDTH), 0.5),
        'state_conv': nrm((L, DEC_BATCH, CONV_WIDTH - 1, LRU_WIDTH), 1.0),
        'state_gla': nrm((L, DEC_BATCH, GLA_HEADS, GLA_DK, GLA_DV), 1.0),
        'norm_mix': gain((L, D_MODEL)),
        'w_in': nrm((L, D_MODEL, IN_WIDTH), D_MODEL ** -0.5),
        's5_lambda_re': -0.5 + nrm((L, S5_GROUPS, S5_STATE), 0.01),
        's5_lambda_im': math.pi * n_idx + nrm((L, S5_GROUPS, S5_STATE), 0.01),
        's5_log_dt': jax.random.uniform(next(ks), (L, S5_GROUPS), F32, math.log(S5_DT_MIN), math.log(S5_DT_MAX)),
        's5_b_re': nrm((L, S5_GROUPS, S5_STATE, S5_GROUP), (2 * S5_GROUP) ** -0.5),
        's5_b_im': nrm((L, S5_GROUPS, S5_STATE, S5_GROUP), (2 * S5_GROUP) ** -0.5),
        's5_c_re': nrm((L, S5_GROUPS, S5_GROUP, S5_STATE), S5_STATE ** -0.5),
        's5_c_im': nrm((L, S5_GROUPS, S5_GROUP, S5_STATE), S5_STATE ** -0.5),
        's5_d': nrm((L, MIX_WIDTH), 1.0),
        's5_w_glu': nrm((L, MIX_WIDTH, MIX_WIDTH), MIX_WIDTH ** -0.5),
        's5_b_glu': nrm((L, MIX_WIDTH), 0.01),
        'ret_norm': gain((L, MIX_WIDTH)),
        'lru_conv_w': nrm((L, CONV_WIDTH, LRU_WIDTH), CONV_WIDTH ** -0.5),
        'lru_conv_b': nrm((L, LRU_WIDTH), 0.01),
        'lru_w_a': nrm((L, LRU_BLOCKS, LRU_BLOCK, LRU_BLOCK), LRU_BLOCK ** -0.5),
        'lru_b_a': nrm((L, LRU_WIDTH), 0.01),
        'lru_w_x': nrm((L, LRU_BLOCKS, LRU_BLOCK, LRU_BLOCK), LRU_BLOCK ** -0.5),
        'lru_b_x': nrm((L, LRU_WIDTH), 0.01),
        'lru_lambda': jnp.log(a0) - jnp.log1p(-a0),
        'gla_w_a2': nrm((L, GLA_GATE_RANK, GLA_HEADS * GLA_DK), GLA_GATE_RANK ** -0.5),
        'gla_b_a2': nrm((L, GLA_HEADS * GLA_DK), 0.5),
        'gla_norm': gain((L, MIX_WIDTH)),
        'w_branch': nrm((L, N_BRANCH, MIX_WIDTH, D_MODEL), MIX_WIDTH ** -0.5),
        'w_out': nrm((L, D_MODEL, D_MODEL), D_MODEL ** -0.5),
        'norm_ffn': gain((L, D_MODEL)),
        'peer_w_q': nrm((L, D_MODEL, PEER_HEADS * PEER_DQ), D_MODEL ** -0.5),
        'peer_sub_keys': nrm((L, PEER_HEADS, 2, PEER_NKEYS, PEER_DQ // 2), (PEER_DQ // 2) ** -0.5),
        'peer_u': nrm((L, PEER_EXPERTS, D_MODEL), D_MODEL ** -0.5),
        'peer_v': nrm((L, PEER_EXPERTS, D_MODEL), PEER_HEADS ** -0.5),
        'norm_final': gain((D_MODEL,)),
    }


def reference(x_prompt, x_sample, state_s5_re, state_s5_im, state_ret, state_lru, state_conv, state_gla,
              norm_mix, w_in, s5_lambda_re, s5_lambda_im, s5_log_dt, s5_b_re, s5_b_im, s5_c_re, s5_c_im,
              s5_d, s5_w_glu, s5_b_glu, ret_norm, lru_conv_w, lru_conv_b, lru_w_a, lru_b_a, lru_w_x, lru_b_x,
              lru_lambda, gla_w_a2, gla_b_a2, gla_norm, w_branch, w_out, norm_ffn, peer_w_q, peer_sub_keys,
              peer_u, peer_v, norm_final):
    pos_p = jnp.arange(x_prompt.shape[1], dtype=jnp.int32)
    pos_s = PAST_LEN + jnp.arange(x_sample.shape[1], dtype=jnp.int32)
    hp = x_prompt
    hs = x_sample
    new_p = []
    new_s = []
    for l in range(DEPTH):
        lp = dict(norm_mix=norm_mix[l], w_in=w_in[l], s5_lambda_re=s5_lambda_re[l], s5_lambda_im=s5_lambda_im[l],
                  s5_log_dt=s5_log_dt[l], s5_b_re=s5_b_re[l], s5_b_im=s5_b_im[l], s5_c_re=s5_c_re[l],
                  s5_c_im=s5_c_im[l], s5_d=s5_d[l], s5_w_glu=s5_w_glu[l], s5_b_glu=s5_b_glu[l],
                  ret_norm=ret_norm[l], lru_conv_w=lru_conv_w[l], lru_conv_b=lru_conv_b[l], lru_w_a=lru_w_a[l],
                  lru_b_a=lru_b_a[l], lru_w_x=lru_w_x[l], lru_b_x=lru_b_x[l], lru_lambda=lru_lambda[l],
                  gla_w_a2=gla_w_a2[l], gla_b_a2=gla_b_a2[l], gla_norm=gla_norm[l], w_branch=w_branch[l],
                  w_out=w_out[l], norm_ffn=norm_ffn[l], peer_w_q=peer_w_q[l], peer_sub_keys=peer_sub_keys[l],
                  peer_u=peer_u[l], peer_v=peer_v[l])
        hp, sp = _layer(hp, pos_p, _zero_state(x_prompt.shape[0]), lp)
        st_l = (state_s5_re[l], state_s5_im[l], state_ret[l], state_lru[l], state_conv[l], state_gla[l])
        hs, ss = _layer(hs, pos_s, st_l, lp)
        new_p.append(sp)
        new_s.append(ss)
    y_prompt = _rmsnorm(hp, norm_final)
    y_sample = _rmsnorm(hs, norm_final)
    p_s5_re = jnp.stack([s[0] for s in new_p])
    p_s5_im = jnp.stack([s[1] for s in new_p])
    p_ret = jnp.stack([s[2] for s in new_p])
    p_lru = jnp.stack([s[3] for s in new_p])
    p_conv = jnp.stack([s[4] for s in new_p])
    p_gla = jnp.stack([s[5] for s in new_p])
    s_s5_re = jnp.stack([s[0] for s in new_s])
    s_s5_im = jnp.stack([s[1] for s in new_s])
    s_ret = jnp.stack([s[2] for s in new_s])
    s_lru = jnp.stack([s[3] for s in new_s])
    s_conv = jnp.stack([s[4] for s in new_s])
    s_gla = jnp.stack([s[5] for s in new_s])
    return (y_prompt, y_sample, p_s5_re, p_s5_im, p_ret, p_lru, p_conv, p_gla,
            s_s5_re, s_s5_im, s_ret, s_lru, s_conv, s_gla)
```

```python
import functools
import math

import jax
import jax.numpy as jnp
from jax import lax
from jax.experimental import pallas as pl
from jax.experimental.pallas import tpu as pltpu

F32 = jnp.float32
BF16 = jnp.bfloat16

D_MODEL = 1024
MIX = 512
NORM_EPS = 1e-6
PAST_LEN = 16384
S5_GROUPS = 32
S5_GROUP = 16
S5_STATE = 64
S5_CHUNK = 16
S5_PAIRS = S5_GROUPS // 2
HEADS = 4
DK = 64
DV = 128
QK = HEADS * DK
ROPE_BASE = 10000.0
RET_CHUNK = 256
GLA_CHUNK = 64
GLA_BLOCK = 256
GLA_RANK = 16
GLA_TAU = 16.0
LRU_BLOCKS = 8
LRU_C = 8.0
CONV_W = 4
PEER_HEADS = 8
PEER_NK = 128
PEER_EXPERTS = PEER_NK * PEER_NK
PEER_TOPK = 16
PEER_EBLK = 1024
PEER_ROWS = PEER_EBLK // PEER_NK

COL_MG = 0
COL_U = 4096
COL_RQ = 4608
COL_RK = 4864
COL_RV = 5120
COL_RG = 5632
COL_LX = 6144
COL_LG = 6656
COL_GQ = 7168
COL_GK = 7424
COL_GV = 7680
COL_GG = 8192
IN_MAIN = 8704
IN_TILE = 512

VMEM_LIMIT = 52 * 1024 * 1024


def _cparams(sem):
    return pltpu.CompilerParams(dimension_semantics=sem, vmem_limit_bytes=VMEM_LIMIT)


def _dot(a, b):
    return jnp.dot(a, b, preferred_element_type=F32)


def _dot_nt(a, b):
    return lax.dot_general(a, b, (((1,), (1,)), ((), ())), preferred_element_type=F32)


def _dot_tn(a, b):
    return lax.dot_general(a, b, (((0,), (0,)), ((), ())), preferred_element_type=F32)


def _sigmoid(x):
    return 1.0 / (1.0 + jnp.exp(-x))


def _silu(x):
    return x * _sigmoid(x)


def _gelu(x):
    return 0.5 * x * (1.0 + jnp.tanh(math.sqrt(2.0 / math.pi) * (x + 0.044715 * (x * x * x))))


def _softplus(x):
    return jnp.maximum(x, 0.0) + jnp.log1p(jnp.exp(-jnp.abs(x)))


def _rms(x, g):
    return x * lax.rsqrt(jnp.mean(x * x, axis=-1, keepdims=True) + NORM_EPS) * g


def _inproj_kernel(x_ref, g_ref, w_ref, wga_ref, o_ref, ga_ref, n_ref):
    @pl.when(pl.program_id(1) == 0)
    def _():
        n = _rms(x_ref[...], g_ref[...]).astype(BF16)
        n_ref[...] = n
        ga_ref[...] = _dot(n, wga_ref[...])

    o_ref[...] = _dot(n_ref[...], w_ref[...])


def _inproj(x, g, w, wga, tm):
    t = x.shape[0]
    return pl.pallas_call(
        _inproj_kernel,
        out_shape=(jax.ShapeDtypeStruct((t, IN_MAIN), F32), jax.ShapeDtypeStruct((t, 128), F32)),
        grid=(t // tm, IN_MAIN // IN_TILE),
        in_specs=[pl.BlockSpec((tm, D_MODEL), lambda i, j: (i, 0)),
                  pl.BlockSpec((1, D_MODEL), lambda i, j: (0, 0)),
                  pl.BlockSpec((D_MODEL, IN_TILE), lambda i, j: (0, j)),
                  pl.BlockSpec((D_MODEL, 128), lambda i, j: (0, 0))],
        out_specs=(pl.BlockSpec((tm, IN_TILE), lambda i, j: (i, j)),
                   pl.BlockSpec((tm, 128), lambda i, j: (i, 0))),
        scratch_shapes=[pltpu.VMEM((tm, D_MODEL), BF16)],
        compiler_params=_cparams(("parallel", "arbitrary")),
        name="inproj",
    )(x, g, w, wga)


def _prep_kernel(rq_ref, rk_ref, ga_ref, cos_ref, sin_ref, wa2_ref, ba2_ref, qr_ref, kr_ref, la_ref):
    cos = cos_ref[...]
    sin = sin_ref[...]
    q = rq_ref[...]
    k = rk_ref[...]
    q1, q2 = q[:, :128], q[:, 128:]
    k1, k2 = k[:, :128], k[:, 128:]
    qr_ref[:, :128] = q1 * cos - q2 * sin
    qr_ref[:, 128:] = q1 * sin + q2 * cos
    kr_ref[:, :128] = (k1 * cos - k2 * sin) * (DK ** -0.5)
    kr_ref[:, 128:] = (k1 * sin + k2 * cos) * (DK ** -0.5)
    z = _dot(ga_ref[...].astype(BF16), wa2_ref[...]) + ba2_ref[...]
    la_ref[...] = (jnp.minimum(z, 0.0) - jnp.log1p(jnp.exp(-jnp.abs(z)))) * (1.0 / GLA_TAU)


def _prep(p, ga, cos, sin, wa2, ba2, tm):
    t = p.shape[0]
    n_pos = cos.shape[0] // tm
    return pl.pallas_call(
        _prep_kernel,
        out_shape=(jax.ShapeDtypeStruct((t, QK), F32),) * 3,
        grid=(t // tm,),
        in_specs=[pl.BlockSpec((tm, QK), lambda i: (i, COL_RQ // QK)),
                  pl.BlockSpec((tm, QK), lambda i: (i, COL_RK // QK)),
                  pl.BlockSpec((tm, 128), lambda i: (i, 0)),
                  pl.BlockSpec((tm, 128), lambda i: (i % n_pos, 0)),
                  pl.BlockSpec((tm, 128), lambda i: (i % n_pos, 0)),
                  pl.BlockSpec((128, QK), lambda i: (0, 0)),
                  pl.BlockSpec((1, QK), lambda i: (0, 0))],
        out_specs=(pl.BlockSpec((tm, QK), lambda i: (i, 0)),) * 3,
        compiler_params=_cparams(("parallel",)),
        name="prep",
    )(p, p, ga, cos, sin, wa2, ba2)


def _s5_prompt_kernel(u_ref, m_ref, wzr_ref, wzi_ref, wyr_ref, wyi_ref, ar_ref, ai_ref,
                      y_ref, sr_ref, si_ref, zr_ref, zi_ref, *, n_chunks, bsz):
    u = u_ref[0].astype(BF16)
    zr_ref[...] = _dot(u, wzr_ref[0])
    zi_ref[...] = _dot(u, wzi_ref[0])
    a_re = jnp.broadcast_to(ar_ref[0], (bsz, 128))
    a_im = jnp.broadcast_to(ai_ref[0], (bsz, 128))

    def body(n, carry):
        s_re, s_im = carry
        rows = pl.ds(pl.multiple_of(n * bsz, bsz), bsz)
        z_re = zr_ref[rows, :]
        z_im = zi_ref[rows, :]
        zr_ref[rows, :] = s_re
        zi_ref[rows, :] = s_im
        return (a_re * s_re - a_im * s_im + z_re, a_re * s_im + a_im * s_re + z_im)

    zero = jnp.zeros((bsz, 128), F32)
    s_re, s_im = lax.fori_loop(0, n_chunks, body, (zero, zero))
    sr_ref[0] = s_re
    si_ref[0] = s_im
    y_ref[0] = (_dot(u, m_ref[0]) + _dot(zr_ref[...].astype(BF16), wyr_ref[0])
                + _dot(zi_ref[...].astype(BF16), wyi_ref[0]))


def _s5_prompt(u, w, bsz, seq):
    n_chunks = seq // S5_CHUNK
    rows = n_chunks * bsz
    ug = (u.reshape(bsz, n_chunks, S5_CHUNK, S5_PAIRS, 2, S5_GROUP)
          .transpose(3, 1, 0, 4, 2, 5).reshape(S5_PAIRS, rows, 512))
    spec3 = lambda a, b: pl.BlockSpec((1, a, b), lambda j: (j, 0, 0))
    y, s_re, s_im = pl.pallas_call(
        functools.partial(_s5_prompt_kernel, n_chunks=n_chunks, bsz=bsz),
        out_shape=(jax.ShapeDtypeStruct((S5_PAIRS, rows, 512), F32),
                   jax.ShapeDtypeStruct((S5_PAIRS, bsz, 128), F32),
                   jax.ShapeDtypeStruct((S5_PAIRS, bsz, 128), F32)),
        grid=(S5_PAIRS,),
        in_specs=[spec3(rows, 512), spec3(512, 512), spec3(512, 128), spec3(512, 128),
                  spec3(128, 512), spec3(128, 512), spec3(1, 128), spec3(1, 128)],
        out_specs=(spec3(rows, 512), spec3(bsz, 128), spec3(bsz, 128)),
        scratch_shapes=[pltpu.VMEM((rows, 128), F32), pltpu.VMEM((rows, 128), F32)],
        compiler_params=_cparams(("parallel",)),
        name="s5_prompt",
    )(ug, w["m"], w["wz_re"], w["wz_im"], w["wy_re"], w["wy_im"], w["a16_re"], w["a16_im"])
    y = (y.reshape(S5_PAIRS, n_chunks, bsz, 2, S5_CHUNK, S5_GROUP)
         .transpose(2, 1, 4, 0, 3, 5).reshape(bsz * seq, MIX))
    to_state = lambda s: s.reshape(S5_PAIRS, bsz, 2, S5_STATE).transpose(1, 0, 2, 3).reshape(bsz, S5_GROUPS, S5_STATE)
    return y, to_state(s_re), to_state(s_im)


def _s5_step_kernel(u_ref, hr_ref, hi_ref, bre_ref, bim_ref, abr_ref, abi_ref, cre_ref, cim_ref,
                    y_ref, sr_ref, si_ref):
    u = u_ref[...].astype(BF16)
    h_re = hr_ref[...]
    h_im = hi_ref[...]
    ab_re = abr_ref[...]
    ab_im = abi_ref[...]
    s_re = _dot(u, bre_ref[...]) + (ab_re * h_re - ab_im * h_im)
    s_im = _dot(u, bim_ref[...]) + (ab_re * h_im + ab_im * h_re)
    sr_ref[...] = s_re
    si_ref[...] = s_im
    y_ref[...] = _dot(s_re.astype(BF16), cre_ref[...]) - _dot(s_im.astype(BF16), cim_ref[...])


def _s5_step(p, h_re, h_im, w):
    t = p.shape[0]
    ns = S5_GROUPS * S5_STATE
    full = lambda a, b: pl.BlockSpec((a, b), lambda i: (0, 0))
    return pl.pallas_call(
        _s5_step_kernel,
        out_shape=(jax.ShapeDtypeStruct((t, MIX), F32), jax.ShapeDtypeStruct((t, ns), F32),
                   jax.ShapeDtypeStruct((t, ns), F32)),
        grid=(1,),
        in_specs=[pl.BlockSpec((t, MIX), lambda i: (0, COL_U // MIX)), full(t, ns), full(t, ns),
                  full(MIX, ns), full(MIX, ns), full(1, ns), full(1, ns), full(ns, MIX), full(ns, MIX)],
        out_specs=(full(t, MIX), full(t, ns), full(t, ns)),
        compiler_params=_cparams(("arbitrary",)),
        name="s5_step",
    )(p, h_re, h_im, w["bd_re"], w["bd_im"], w["ab_re"], w["ab_im"], w["cd_re"], w["cd_im"])


def _lru_kernel(x_ref, g_ref, h0_ref, c0_ref, cw_ref, cb_ref, wa_ref, ba_ref, wx_ref, bx_ref, lam_ref,
                y_ref, hf_ref, cf_ref, xbuf, hs, abuf, bbuf, *, steps, bsz):
    rows = steps * bsz
    halo = (CONV_W - 1) * bsz

    @pl.when(pl.program_id(0) == 0)
    def _():
        xbuf[0:halo, :] = c0_ref[...]
        hs[...] = h0_ref[...]

    xbuf[halo:halo + rows, :] = x_ref[...]
    xc = cb_ref[...] + xbuf[0:rows, :] * cw_ref[0:1, :]
    for j in range(1, CONV_W):
        xc = xc + xbuf[j * bsz:j * bsz + rows, :] * cw_ref[j:j + 1, :]
    xcb = xc.astype(BF16)
    r = _sigmoid(_dot(xcb, wa_ref[...]) + ba_ref[...])
    i = _sigmoid(_dot(xcb, wx_ref[...]) + bx_ref[...])
    log_a = (-LRU_C) * r * _softplus(-lam_ref[...])
    a = jnp.exp(log_a)
    abuf[...] = a
    bbuf[...] = jnp.sqrt(-jnp.tanh(log_a) * (a * a + 1.0)) * (i * xc)

    def body(t, h):
        sl = pl.ds(pl.multiple_of(t * bsz, bsz), bsz)
        h = abuf[sl, :] * h + bbuf[sl, :]
        bbuf[sl, :] = h
        return h

    h = lax.fori_loop(0, steps, body, hs[...])
    hs[...] = h
    hf_ref[...] = h
    y_ref[...] = bbuf[...] * _gelu(g_ref[...])
    tail = xbuf[rows:rows + halo, :]
    xbuf[0:halo, :] = tail
    cf_ref[...] = tail


def _lru(x, g, h0, c0, w, bsz, steps):
    t = x.shape[0]
    rows = steps * bsz
    halo = (CONV_W - 1) * bsz
    full = lambda a, b: pl.BlockSpec((a, b), lambda i: (0, 0))
    return pl.pallas_call(
        functools.partial(_lru_kernel, steps=steps, bsz=bsz),
        out_shape=(jax.ShapeDtypeStruct((t, MIX), F32), jax.ShapeDtypeStruct((bsz, MIX), F32),
                   jax.ShapeDtypeStruct((halo, MIX), F32)),
        grid=(t // rows,),
        in_specs=[pl.BlockSpec((rows, MIX), lambda i: (i, 0)), pl.BlockSpec((rows, MIX), lambda i: (i, 0)),
                  full(bsz, MIX), full(halo, MIX), full(CONV_W, MIX), full(1, MIX),
                  full(MIX, MIX), full(1, MIX), full(MIX, MIX), full(1, MIX), full(1, MIX)],
        out_specs=(pl.BlockSpec((rows, MIX), lambda i: (i, 0)), full(bsz, MIX), full(halo, MIX)),
        scratch_shapes=[pltpu.VMEM((rows + halo, MIX), F32), pltpu.VMEM((bsz, MIX), F32),
                        pltpu.VMEM((rows, MIX), F32), pltpu.VMEM((rows, MIX), F32)],
        compiler_params=_cparams(("arbitrary",)),
        name="rglru",
    )(x, g, h0, c0, w["conv_w"], w["conv_b"], w["wa"], w["ba"], w["wx"], w["bx"], w["lam"])


def _ret_prompt_kernel(q_ref, k_ref, v_ref, dec_ref, wq_ref, wk_ref, gc_ref, hm_ref, o_ref, sf_ref, s_ref):
    @pl.when(pl.program_id(1) == 0)
    def _():
        s_ref[...] = jnp.zeros_like(s_ref)

    q = q_ref[...]
    k = k_ref[...]
    kb = k.astype(BF16)
    vb = v_ref[...].astype(BF16)
    sb = s_ref[...].astype(BF16)
    qw = q * wq_ref[...]
    for h in range(HEADS):
        hm = hm_ref[h]
        vs = slice(h * DV, (h + 1) * DV)
        sc = _dot_nt((q * hm).astype(BF16), kb) * dec_ref[h]
        o_ref[:, vs] = _dot(sc.astype(BF16), vb[:, vs]) + _dot((qw * hm).astype(BF16), sb[:, vs])
    s_new = s_ref[...] * gc_ref[...] + _dot_tn((k * wk_ref[...]).astype(BF16), vb)
    s_ref[...] = s_new
    sf_ref[0] = s_new


def _ret_prompt(qr, kr, p, c, bsz, seq):
    ch = RET_CHUNK
    n = seq // ch
    full2 = lambda a, b: pl.BlockSpec((a, b), lambda bi, ci: (0, 0))
    o, sf = pl.pallas_call(
        _ret_prompt_kernel,
        out_shape=(jax.ShapeDtypeStruct((bsz * seq, MIX), F32), jax.ShapeDtypeStruct((bsz, QK, MIX), F32)),
        grid=(bsz, n),
        in_specs=[pl.BlockSpec((ch, QK), lambda bi, ci: (bi * n + ci, 0)),
                  pl.BlockSpec((ch, QK), lambda bi, ci: (bi * n + ci, 0)),
                  pl.BlockSpec((ch, MIX), lambda bi, ci: (bi * n + ci, COL_RV // MIX)),
                  pl.BlockSpec((HEADS, ch, ch), lambda bi, ci: (0, 0, 0)),
                  full2(ch, QK), full2(ch, QK), full2(1, MIX),
                  pl.BlockSpec((HEADS, 1, QK), lambda bi, ci: (0, 0, 0))],
        out_specs=(pl.BlockSpec((ch, MIX), lambda bi, ci: (bi * n + ci, 0)),
                   pl.BlockSpec((1, QK, MIX), lambda bi, ci: (bi, 0, 0))),
        scratch_shapes=[pltpu.VMEM((QK, MIX), F32)],
        compiler_params=_cparams(("parallel", "arbitrary")),
        name="ret_prompt",
    )(qr, kr, p, c["dec"], c["wq"], c["wk"], c["gc"], c["hmask"])
    s5d = sf.reshape(bsz, 2, HEADS, 32, HEADS, DV)
    st = jnp.stack([s5d[:, :, h, :, h, :] for h in range(HEADS)], axis=1)
    return o, st.reshape(bsz, HEADS, DK, DV)


def _gla_prompt_kernel(q_ref, k_ref, v_ref, la_ref, tril_ref, hm_ref, o_ref, sf_ref, s_ref):
    @pl.when(pl.program_id(1) == 0)
    def _():
        s_ref[...] = jnp.zeros_like(s_ref)

    tril = tril_ref[...]
    for c in range(GLA_BLOCK // GLA_CHUNK):
        rs = slice(c * GLA_CHUNK, (c + 1) * GLA_CHUNK)
        b = jnp.dot(tril, la_ref[rs, :], preferred_element_type=F32, precision=lax.Precision.HIGHEST)
        b_last = b[GLA_CHUNK - 1:GLA_CHUNK, :]
        q = q_ref[rs, :] * (DK ** -0.5)
        k = k_ref[rs, :]
        q_t = q * jnp.exp(b)
        k_t = (k * jnp.exp(-b)).astype(BF16)
        k_s = (k * jnp.exp(b_last - b)).astype(BF16)
        vb = v_ref[rs, :].astype(BF16)
        sb = s_ref[...].astype(BF16)
        for h in range(HEADS):
            qm = (q_t * hm_ref[h]).astype(BF16)
            vs = slice(h * DV, (h + 1) * DV)
            sc = jnp.where(tril > 0.0, _dot_nt(qm, k_t), 0.0)
            o_ref[rs, vs] = _dot(sc.astype(BF16), vb[:, vs]) + _dot_nt(qm, sb[vs, :])
        s_ref[...] = s_ref[...] * jnp.exp(b_last) + _dot_tn(vb, k_s)
    sf_ref[0] = s_ref[...]


def _gla_prompt(p, la, c, bsz, seq):
    blk = GLA_BLOCK
    n = seq // blk
    o, sf = pl.pallas_call(
        _gla_prompt_kernel,
        out_shape=(jax.ShapeDtypeStruct((bsz * seq, MIX), F32), jax.ShapeDtypeStruct((bsz, MIX, QK), F32)),
        grid=(bsz, n),
        in_specs=[pl.BlockSpec((blk, QK), lambda bi, ci: (bi * n + ci, COL_GQ // QK)),
                  pl.BlockSpec((blk, QK), lambda bi, ci: (bi * n + ci, COL_GK // QK)),
                  pl.BlockSpec((blk, MIX), lambda bi, ci: (bi * n + ci, COL_GV // MIX)),
                  pl.BlockSpec((blk, QK), lambda bi, ci: (bi * n + ci, 0)),
                  pl.BlockSpec((GLA_CHUNK, GLA_CHUNK), lambda bi, ci: (0, 0)),
                  pl.BlockSpec((HEADS, 1, QK), lambda bi, ci: (0, 0, 0))],
        out_specs=(pl.BlockSpec((blk, MIX), lambda bi, ci: (bi * n + ci, 0)),
                   pl.BlockSpec((1, MIX, QK), lambda bi, ci: (bi, 0, 0))),
        scratch_shapes=[pltpu.VMEM((MIX, QK), F32)],
        compiler_params=_cparams(("parallel", "arbitrary")),
        name="gla_prompt",
    )(p, p, p, la, c["tril"], c["hmask"])
    s5d = sf.reshape(bsz, HEADS, DV, HEADS, DK)
    st = jnp.stack([s5d[:, h, :, h, :] for h in range(HEADS)], axis=1)
    return o, st.transpose(0, 1, 3, 2)


STEP_GROUP = 8


def _mat_step_kernel(rq_ref, rk_ref, rv_ref, rs_ref, gq_ref, gk_ref, ga_ref, gv_ref, gs_ref, rdec_ref,
                     ro_ref, rso_ref, go_ref, gso_ref):
    rq = rq_ref[0]
    rk = rk_ref[0]
    gq = gq_ref[0] * (DK ** -0.5)
    gk = gk_ref[0]
    ga = jnp.exp(ga_ref[0])
    for i in range(STEP_GROUP):
        for h in range(HEADS):
            ks = slice(h * DK, (h + 1) * DK)
            vs = slice(h * DV, (h + 1) * DV)
            s1 = rs_ref[i, h] * rdec_ref[h] + rk[ks, i:i + 1] * rv_ref[i:i + 1, vs]
            rso_ref[i, h] = s1
            ro_ref[i:i + 1, vs] = jnp.sum(rq[ks, i:i + 1] * s1, axis=0, keepdims=True)
            s1 = gs_ref[i, h] * ga[ks, i:i + 1] + gk[ks, i:i + 1] * gv_ref[i:i + 1, vs]
            gso_ref[i, h] = s1
            go_ref[i:i + 1, vs] = jnp.sum(gq[ks, i:i + 1] * s1, axis=0, keepdims=True)


def _mat_step(qr, kr, la, p, s_ret, s_gla, rdec):
    t = p.shape[0]
    g = STEP_GROUP
    ng = t // g

    def tr(x):
        return x.reshape(ng, g, QK).transpose(0, 2, 1)

    def unsplit(x):
        return x.reshape(t, 2, HEADS, 32).transpose(0, 2, 1, 3).reshape(t, QK)

    gq = p[:, COL_GQ:COL_GQ + QK]
    gk = p[:, COL_GK:COL_GK + QK]
    tspec = pl.BlockSpec((1, QK, g), lambda i: (i, 0, 0))
    sspec = pl.BlockSpec((g, HEADS, DK, DV), lambda i: (i, 0, 0, 0))
    vspec = lambda col: pl.BlockSpec((g, MIX), lambda i: (i, col // MIX))
    ospec = pl.BlockSpec((g, MIX), lambda i: (i, 0))
    return pl.pallas_call(
        _mat_step_kernel,
        out_shape=(jax.ShapeDtypeStruct((t, MIX), F32), jax.ShapeDtypeStruct(s_ret.shape, F32),
                   jax.ShapeDtypeStruct((t, MIX), F32), jax.ShapeDtypeStruct(s_gla.shape, F32)),
        grid=(ng,),
        in_specs=[tspec, tspec, vspec(COL_RV), sspec, tspec, tspec, tspec, vspec(COL_GV), sspec,
                  pl.BlockSpec((HEADS, 1, DV), lambda i: (0, 0, 0))],
        out_specs=(ospec, sspec, ospec, sspec),
        compiler_params=_cparams(("parallel",)),
        name="mat_step",
    )(tr(unsplit(qr)), tr(unsplit(kr)), p, s_ret, tr(gq), tr(gk), tr(la), p, s_gla, rdec)


def _head_norm(o, g):
    parts = []
    for h in range(HEADS):
        oh = o[:, h * DV:(h + 1) * DV]
        mu = jnp.mean(oh, axis=-1, keepdims=True)
        d = oh - mu
        var = jnp.mean(d * d, axis=-1, keepdims=True)
        parts.append(d * lax.rsqrt(var + NORM_EPS))
    return jnp.concatenate(parts, axis=-1) * g


def _merge_kernel(x_ref, mg_ref, u_ref, rg_ref, gg_ref, ys_ref, ro_ref, yc_ref, go_ref,
                  d_ref, wglu_ref, bglu_ref, gret_ref, ggla_ref, wb_ref, wout_ref, gffn_ref,
                  h_ref, n_ref):
    ya = _gelu(ys_ref[...] + d_ref[...] * u_ref[...])
    ya = ya * _sigmoid(_dot(ya.astype(BF16), wglu_ref[...]) + bglu_ref[...])
    yb = _silu(rg_ref[...]) * _head_norm(ro_ref[...], gret_ref[...])
    yc = yc_ref[...]
    yd = _silu(gg_ref[...]) * _head_norm(go_ref[...], ggla_ref[...])
    m = None
    for i, y in enumerate((ya, yb, yc, yd)):
        z = _dot(y.astype(BF16), wb_ref[i])
        gz = _sigmoid(mg_ref[:, i * D_MODEL:(i + 1) * D_MODEL]) * z
        m = gz if m is None else m + gz
    h = x_ref[...] + _dot(m.astype(BF16), wout_ref[...])
    h_ref[...] = h
    n_ref[...] = _rms(h, gffn_ref[...]).astype(BF16)


def _merge(x, p, ys, ro, yc, go, w, tm):
    t = x.shape[0]
    row = lambda width, col=0: pl.BlockSpec((tm, width), lambda i: (i, col // width))
    full = lambda a, b: pl.BlockSpec((a, b), lambda i: (0, 0))
    return pl.pallas_call(
        _merge_kernel,
        out_shape=(jax.ShapeDtypeStruct((t, D_MODEL), F32), jax.ShapeDtypeStruct((t, D_MODEL), BF16)),
        grid=(t // tm,),
        in_specs=[row(D_MODEL), row(4 * D_MODEL, COL_MG), row(MIX, COL_U), row(MIX, COL_RG), row(MIX, COL_GG),
                  row(MIX), row(MIX), row(MIX), row(MIX),
                  full(1, MIX), full(MIX, MIX), full(1, MIX), full(1, MIX), full(1, MIX),
                  pl.BlockSpec((4, MIX, D_MODEL), lambda i: (0, 0, 0)), full(D_MODEL, D_MODEL), full(1, D_MODEL)],
        out_specs=(row(D_MODEL), row(D_MODEL)),
        compiler_params=_cparams(("parallel",)),
        name="merge",
    )(x, p, p, p, p, ys, ro, yc, go, w["s5_d"], w["w_glu"], w["b_glu"], w["ret_norm"], w["gla_norm"],
      w["w_branch"], w["w_out"], w["norm_ffn"])


def _scores_kernel(n_ref, wq_ref, sk_ref, st_ref):
    q = _dot(n_ref[...], wq_ref[...]).astype(BF16)
    for hp in range(2 * PEER_HEADS):
        st_ref[hp] = _dot_nt(sk_ref[hp], q[:, hp * PEER_NK:(hp + 1) * PEER_NK])


def _scores(n2, wq, sk, tm):
    t = n2.shape[0]
    return pl.pallas_call(
        _scores_kernel,
        out_shape=jax.ShapeDtypeStruct((2 * PEER_HEADS, PEER_NK, t), F32),
        grid=(t // tm,),
        in_specs=[pl.BlockSpec((tm, D_MODEL), lambda i: (i, 0)),
                  pl.BlockSpec((D_MODEL, 2 * PEER_HEADS * PEER_NK), lambda i: (0, 0)),
                  pl.BlockSpec((2 * PEER_HEADS, PEER_NK, PEER_NK), lambda i: (0, 0, 0))],
        out_specs=pl.BlockSpec((2 * PEER_HEADS, PEER_NK, tm), lambda i: (0, 0, i)),
        compiler_params=_cparams(("parallel",)),
        name="peer_scores",
    )(n2, wq, sk)


def _top_desc(cur, dst_ref):
    for r in range(PEER_TOPK):
        m = jnp.max(cur, axis=0, keepdims=True)
        dst_ref[r:r + 1, :] = m
        cur = jnp.where(cur >= m, -jnp.inf, cur)


def _peer_kernel(st_ref, n_ref, h_ref, u_ref, vt_ref, gf_ref, o_ref,
                 e0_ref, e1_ref, tau_ref, a0_ref, a1_ref, cand_ref, g_ref, acc_ref, *, final_norm):
    j = pl.program_id(1)

    @pl.when(j == 0)
    def _():
        acc_ref[...] = jnp.zeros_like(acc_ref)
        for h in range(PEER_HEADS):
            s0 = st_ref[2 * h]
            s1 = st_ref[2 * h + 1]
            _top_desc(s0, a0_ref)
            _top_desc(s1, a1_ref)
            a1 = a1_ref[...]
            for a in range(PEER_TOPK):
                cand_ref[a * PEER_TOPK:(a + 1) * PEER_TOPK, :] = a0_ref[a:a + 1, :] + a1
            cand = cand_ref[...]
            cur = cand
            for r in range(PEER_TOPK):
                tau = jnp.max(cur, axis=0, keepdims=True)
                cur = jnp.where(cur >= tau, -jnp.inf, cur)
            cmax = a0_ref[0:1, :] + a1_ref[0:1, :]
            z = jnp.sum(jnp.where(cand >= tau, jnp.exp(cand - cmax), 0.0), axis=0, keepdims=True)
            e0_ref[h] = jnp.exp(s0 - a0_ref[0:1, :])
            e1_ref[h] = jnp.exp(s1 - a1_ref[0:1, :]) / z
            tau_ref[h] = tau

    for r in range(PEER_ROWS):
        i0 = j * PEER_ROWS + r
        g = None
        for h in range(PEER_HEADS):
            s0row = st_ref[2 * h, pl.ds(i0, 1), :]
            e0row = e0_ref[h, pl.ds(i0, 1), :]
            w = jnp.where(st_ref[2 * h + 1] + s0row >= tau_ref[h], e1_ref[h] * e0row, 0.0)
            g = w if g is None else g + w
        g_ref[r * PEER_NK:(r + 1) * PEER_NK, :] = g

    hid = _gelu(_dot_nt(u_ref[...], n_ref[...]))
    acc_ref[...] += _dot(vt_ref[...], (g_ref[...] * hid).astype(BF16))

    @pl.when(j == pl.num_programs(1) - 1)
    def _():
        out = h_ref[...] + acc_ref[...].T
        if final_norm:
            out = _rms(out, gf_ref[...])
        o_ref[...] = out


def _peer(st, n2, h, u, vt, gfin, tm, final_norm):
    t = n2.shape[0]
    return pl.pallas_call(
        functools.partial(_peer_kernel, final_norm=final_norm),
        out_shape=jax.ShapeDtypeStruct((t, D_MODEL), F32),
        grid=(t // tm, PEER_EXPERTS // PEER_EBLK),
        in_specs=[pl.BlockSpec((2 * PEER_HEADS, PEER_NK, tm), lambda i, j: (0, 0, i)),
                  pl.BlockSpec((tm, D_MODEL), lambda i, j: (i, 0)),
                  pl.BlockSpec((tm, D_MODEL), lambda i, j: (i, 0)),
                  pl.BlockSpec((PEER_EBLK, D_MODEL), lambda i, j: (j, 0)),
                  pl.BlockSpec((D_MODEL, PEER_EBLK), lambda i, j: (0, j)),
                  pl.BlockSpec((1, D_MODEL), lambda i, j: (0, 0))],
        out_specs=pl.BlockSpec((tm, D_MODEL), lambda i, j: (i, 0)),
        scratch_shapes=[pltpu.VMEM((PEER_HEADS, PEER_NK, tm), F32), pltpu.VMEM((PEER_HEADS, PEER_NK, tm), F32),
                        pltpu.VMEM((PEER_HEADS, 1, tm), F32),
                        pltpu.VMEM((PEER_TOPK, tm), F32), pltpu.VMEM((PEER_TOPK, tm), F32),
                        pltpu.VMEM((PEER_TOPK * PEER_TOPK, tm), F32),
                        pltpu.VMEM((PEER_EBLK, tm), F32), pltpu.VMEM((D_MODEL, tm), F32)],
        compiler_params=_cparams(("parallel", "arbitrary")),
        name="peer",
    )(st, n2, h, u, vt, gfin)


def _blockdiag(x, n):
    nm, r, c = x.shape
    m = nm // n
    return jnp.einsum("mgrc,gh->mgrhc", x.reshape(m, n, r, c), jnp.eye(n, dtype=x.dtype)).reshape(m, n * r, n * c)


def _s5_weights(lam_re, lam_im, log_dt, b_re, b_im, c_re, c_im):
    hi = lax.Precision.HIGHEST
    dt = jnp.exp(log_dt)[:, None]
    mag = jnp.exp(lam_re * dt)
    ab_re = mag * jnp.cos(lam_im * dt)
    ab_im = mag * jnp.sin(lam_im * dt)
    den = lam_re * lam_re + lam_im * lam_im
    zr = ((ab_re - 1.0) * lam_re + ab_im * lam_im) / den
    zi = (ab_im * lam_re - (ab_re - 1.0) * lam_im) / den
    bb_re = zr[..., None] * b_re - zi[..., None] * b_im
    bb_im = zr[..., None] * b_im + zi[..., None] * b_re
    kk = jnp.arange(S5_CHUNK + 1, dtype=F32)[:, None, None]
    pmag = jnp.exp(lam_re * dt * kk)
    pw_re = pmag * jnp.cos(lam_im * dt * kk)
    pw_im = pmag * jnp.sin(lam_im * dt * kk)
    cp_re = c_re[None] * pw_re[:, :, None, :] - c_im[None] * pw_im[:, :, None, :]
    cp_im = c_re[None] * pw_im[:, :, None, :] + c_im[None] * pw_re[:, :, None, :]
    kern = (jnp.einsum("tgop,gpc->tgoc", cp_re[:S5_CHUNK], bb_re, precision=hi)
            - jnp.einsum("tgop,gpc->tgoc", cp_im[:S5_CHUNK], bb_im, precision=hi))
    step = jnp.arange(S5_CHUNK)
    lag = step[None, :] - step[:, None]
    m = jnp.where((lag >= 0)[:, :, None, None, None], kern[jnp.clip(lag, 0)], 0.0)
    m = m.transpose(2, 0, 4, 1, 3).reshape(S5_GROUPS, 256, 256)
    rev = pw_re[S5_CHUNK - 1 - step], pw_im[S5_CHUNK - 1 - step]
    wz_re = rev[0][..., None] * bb_re[None] - rev[1][..., None] * bb_im[None]
    wz_im = rev[0][..., None] * bb_im[None] + rev[1][..., None] * bb_re[None]
    to_z = lambda w: w.transpose(1, 0, 3, 2).reshape(S5_GROUPS, 256, S5_STATE)
    wy_re = cp_re[1:].transpose(1, 3, 0, 2).reshape(S5_GROUPS, S5_STATE, 256)
    wy_im = (-cp_im[1:]).transpose(1, 3, 0, 2).reshape(S5_GROUPS, S5_STATE, 256)
    eye = jnp.eye(S5_GROUPS, dtype=F32)
    return {
        "m": _blockdiag(m, 2).astype(BF16),
        "wz_re": _blockdiag(to_z(wz_re), 2).astype(BF16), "wz_im": _blockdiag(to_z(wz_im), 2).astype(BF16),
        "wy_re": _blockdiag(wy_re, 2).astype(BF16), "wy_im": _blockdiag(wy_im, 2).astype(BF16),
        "a16_re": pw_re[S5_CHUNK].reshape(S5_PAIRS, 1, 128), "a16_im": pw_im[S5_CHUNK].reshape(S5_PAIRS, 1, 128),
        "ab_re": ab_re.reshape(1, -1), "ab_im": ab_im.reshape(1, -1),
        "bd_re": jnp.einsum("gpc,gh->gchp", bb_re, eye).reshape(MIX, -1).astype(BF16),
        "bd_im": jnp.einsum("gpc,gh->gchp", bb_im, eye).reshape(MIX, -1).astype(BF16),
        "cd_re": jnp.einsum("gcp,gh->gphc", c_re, eye).reshape(-1, MIX).astype(BF16),
        "cd_im": jnp.einsum("gcp,gh->gphc", c_im, eye).reshape(-1, MIX).astype(BF16),
    }


def _ret_consts():
    ch = RET_CHUNK
    log_g = jnp.log1p(-jnp.exp2(-5.0 - jnp.arange(HEADS, dtype=F32)))
    idx = jnp.arange(ch, dtype=F32)
    diff = idx[:, None] - idx[None, :]
    mask = diff >= 0
    dec = jnp.where(mask[None], jnp.exp(log_g[:, None, None] * jnp.where(mask, diff, 0.0)[None]), 0.0)
    lane_head = (jnp.arange(QK) % 128) // 32
    lg_lane = log_g[lane_head]
    wq = jnp.exp(lg_lane[None, :] * (idx + 1.0)[:, None])
    wk = jnp.exp(lg_lane[None, :] * (ch - 1.0 - idx)[:, None])
    gc = jnp.repeat(jnp.exp(log_g * ch), DV)[None, :]
    hmask = (lane_head[None, :] == jnp.arange(HEADS)[:, None]).astype(F32)[:, None, :]
    rdec = jnp.broadcast_to(jnp.exp(log_g)[:, None, None], (HEADS, 1, DV))
    return {"dec": dec, "wq": wq, "wk": wk, "gc": gc, "hmask": hmask, "rdec": rdec}


def _gla_consts():
    tril = jnp.tril(jnp.ones((GLA_CHUNK, GLA_CHUNK), F32))
    hmask = ((jnp.arange(QK) // DK)[None, :] == jnp.arange(HEADS)[:, None]).astype(F32)[:, None, :]
    return {"tril": tril, "hmask": hmask}


def _rope_tables(pos):
    half = DK // 2
    inv = ROPE_BASE ** (-jnp.arange(half, dtype=F32) / half)
    ang = pos.astype(F32)[:, None] * inv[None, :]
    return jnp.tile(jnp.cos(ang), (1, HEADS)), jnp.tile(jnp.sin(ang), (1, HEADS))


def _layer_weights(lp):
    w_in = lp["w_in"]
    cuts = {}
    off = 0
    for name, size in (("u", 512), ("rq", 256), ("rk", 256), ("rv", 512), ("rg", 512), ("lx", 512), ("lg", 512),
                       ("gq", 256), ("gk", 256), ("gv", 512), ("ga", GLA_RANK), ("gg", 512), ("mg", 4096)):
        cuts[name] = w_in[:, off:off + size]
        off += size

    def split_halves(w):
        return w.reshape(D_MODEL, HEADS, 2, 32).transpose(0, 2, 1, 3).reshape(D_MODEL, QK)

    w_main = jnp.concatenate([cuts["mg"], cuts["u"], split_halves(cuts["rq"]), split_halves(cuts["rk"]), cuts["rv"],
                              cuts["rg"], cuts["lx"], cuts["lg"], cuts["gq"], cuts["gk"], cuts["gv"], cuts["gg"]],
                             axis=1).astype(BF16)
    w_ga = jnp.pad(cuts["ga"], ((0, 0), (0, 128 - GLA_RANK))).astype(BF16)
    row = lambda v: v.reshape(1, -1)
    return {
        "norm_mix": row(lp["norm_mix"]), "w_main": w_main, "w_ga": w_ga,
        "wa2": jnp.pad(lp["gla_w_a2"], ((0, 128 - GLA_RANK), (0, 0))).astype(BF16), "ba2": row(lp["gla_b_a2"]),
        "s5": _s5_weights(lp["s5_lambda_re"], lp["s5_lambda_im"], lp["s5_log_dt"], lp["s5_b_re"], lp["s5_b_im"],
                          lp["s5_c_re"], lp["s5_c_im"]),
        "lru": {"conv_w": lp["lru_conv_w"], "conv_b": row(lp["lru_conv_b"]),
                "wa": _blockdiag(lp["lru_w_a"], LRU_BLOCKS)[0].astype(BF16), "ba": row(lp["lru_b_a"]),
                "wx": _blockdiag(lp["lru_w_x"], LRU_BLOCKS)[0].astype(BF16), "bx": row(lp["lru_b_x"]),
                "lam": row(lp["lru_lambda"])},
        "merge": {"s5_d": row(lp["s5_d"]), "w_glu": lp["s5_w_glu"].astype(BF16), "b_glu": row(lp["s5_b_glu"]),
                  "ret_norm": row(lp["ret_norm"]), "gla_norm": row(lp["gla_norm"]),
                  "w_branch": lp["w_branch"].astype(BF16), "w_out": lp["w_out"].astype(BF16),
                  "norm_ffn": row(lp["norm_ffn"])},
        "peer_wq": lp["peer_w_q"].astype(BF16),
        "peer_sk": lp["peer_sub_keys"].reshape(2 * PEER_HEADS, PEER_NK, PEER_NK).astype(BF16),
        "peer_u": lp["peer_u"].astype(BF16),
        "peer_vt": lp["peer_v"].astype(BF16).T,
    }


def _token_tail(x, p, ys, ro, yc, go, w, gfin, final_norm, tm):
    h, n2 = _merge(x, p, ys, ro, yc, go, w["merge"], tm)
    st = _scores(n2, w["peer_wq"], w["peer_sk"], tm)
    return _peer(st, n2, h, w["peer_u"], w["peer_vt"], gfin, tm, final_norm)


def _layer_prompt(x, w, rc, gc, rope, gfin, final_norm, bsz, seq, tm):
    t = bsz * seq
    p, ga = _inproj(x, w["norm_mix"], w["w_main"], w["w_ga"], tm)
    qr, kr, la = _prep(p, ga, rope[0], rope[1], w["wa2"], w["ba2"], tm)
    ys, s5_re, s5_im = _s5_prompt(p[:, COL_U:COL_U + MIX], w["s5"], bsz, seq)
    ro, s_ret = _ret_prompt(qr, kr, p, rc, bsz, seq)
    go, s_gla = _gla_prompt(p, la, gc, bsz, seq)
    to_tb = lambda a: a.reshape(bsz, seq, MIX).transpose(1, 0, 2).reshape(t, MIX)
    lx = p[:, COL_LX:COL_LX + MIX]
    yc, s_lru, _ = _lru(to_tb(lx), to_tb(p[:, COL_LG:COL_LG + MIX]), jnp.zeros((bsz, MIX), F32),
                        jnp.zeros(((CONV_W - 1) * bsz, MIX), F32), w["lru"], bsz, min(seq, 64))
    yc = yc.reshape(seq, bsz, MIX).transpose(1, 0, 2).reshape(t, MIX)
    s_conv = lx.reshape(bsz, seq, MIX)[:, seq - (CONV_W - 1):, :]
    out = _token_tail(x, p, ys, ro, yc, go, w, gfin, final_norm, tm)
    return out, (s5_re, s5_im, s_ret, s_lru, s_conv, s_gla)


def _layer_sample(x, st, w, rc, rope, gfin, final_norm):
    t = x.shape[0]
    s5_re0, s5_im0, ret0, lru0, conv0, gla0 = st
    p, ga = _inproj(x, w["norm_mix"], w["w_main"], w["w_ga"], t)
    qr, kr, la = _prep(p, ga, rope[0], rope[1], w["wa2"], w["ba2"], t)
    ys, s5_re, s5_im = _s5_step(p, s5_re0.reshape(t, -1), s5_im0.reshape(t, -1), w["s5"])
    ro, s_ret, go, s_gla = _mat_step(qr, kr, la, p, ret0, gla0, rc["rdec"])
    c0 = conv0.transpose(1, 0, 2).reshape((CONV_W - 1) * t, MIX)
    yc, s_lru, cf = _lru(p[:, COL_LX:COL_LX + MIX], p[:, COL_LG:COL_LG + MIX], lru0, c0, w["lru"], t, 1)
    s_conv = cf.reshape(CONV_W - 1, t, MIX).transpose(1, 0, 2)
    out = _token_tail(x, p, ys, ro, yc, go, w, gfin, final_norm, t)
    return out, (s5_re.reshape(s5_re0.shape), s5_im.reshape(s5_im0.shape), s_ret, s_lru, s_conv, s_gla)


_PARAM_NAMES = ("norm_mix", "w_in", "s5_lambda_re", "s5_lambda_im", "s5_log_dt", "s5_b_re", "s5_b_im", "s5_c_re",
                "s5_c_im", "s5_d", "s5_w_glu", "s5_b_glu", "ret_norm", "lru_conv_w", "lru_conv_b", "lru_w_a",
                "lru_b_a", "lru_w_x", "lru_b_x", "lru_lambda", "gla_w_a2", "gla_b_a2", "gla_norm", "w_branch",
                "w_out", "norm_ffn", "peer_w_q", "peer_sub_keys", "peer_u", "peer_v")


def _forward(x_prompt, x_sample, states, params, norm_final, tm):
    bsz, seq, _ = x_prompt.shape
    nsmp = x_sample.shape[0]
    depth = params["w_in"].shape[0]
    rc = _ret_consts()
    gc = _gla_consts()
    rope_p = _rope_tables(jnp.arange(seq, dtype=jnp.int32))
    rope_s = _rope_tables(jnp.full((nsmp,), PAST_LEN, dtype=jnp.int32))
    gfin = norm_final.reshape(1, -1)
    hp = x_prompt.reshape(bsz * seq, D_MODEL)
    hs = x_sample.reshape(nsmp, D_MODEL)
    new_p, new_s = [], []
    for l in range(depth):
        w = _layer_weights({k: v[l] for k, v in params.items()})
        last = l == depth - 1
        hp, sp = _layer_prompt(hp, w, rc, gc, rope_p, gfin, last, bsz, seq, tm)
        hs, ss = _layer_sample(hs, tuple(s[l] for s in states), w, rc, rope_s, gfin, last)
        new_p.append(sp)
        new_s.append(ss)
    outs = [hp.reshape(bsz, seq, D_MODEL), hs.reshape(nsmp, 1, D_MODEL)]
    outs += [jnp.stack([s[i] for s in new_p]) for i in range(6)]
    outs += [jnp.stack([s[i] for s in new_s]) for i in range(6)]
    return tuple(outs)


def kernel(x_prompt, x_sample, state_s5_re, state_s5_im, state_ret, state_lru, state_conv, state_gla, norm_mix, w_in, s5_lambda_re, s5_lambda_im, s5_log_dt, s5_b_re, s5_b_im, s5_c_re, s5_c_im, s5_d, s5_w_glu, s5_b_glu, ret_norm, lru_conv_w, lru_conv_b, lru_w_a, lru_b_a, lru_w_x, lru_b_x, lru_lambda, gla_w_a2, gla_b_a2, gla_norm, w_branch, w_out, norm_ffn, peer_w_q, peer_sub_keys, peer_u, peer_v, norm_final):
    values = (norm_mix, w_in, s5_lambda_re, s5_lambda_im, s5_log_dt, s5_b_re, s5_b_im, s5_c_re, s5_c_im, s5_d,
              s5_w_glu, s5_b_glu, ret_norm, lru_conv_w, lru_conv_b, lru_w_a, lru_b_a, lru_w_x, lru_b_x, lru_lambda,
              gla_w_a2, gla_b_a2, gla_norm, w_branch, w_out, norm_ffn, peer_w_q, peer_sub_keys, peer_u, peer_v)
    params = dict(zip(_PARAM_NAMES, values))
    states = (state_s5_re, state_s5_im, state_ret, state_lru, state_conv, state_gla)
    return _forward(x_prompt, x_sample, states, params, norm_final, 256)
```

```python
import functools
import math

import jax
import jax.numpy as jnp
from jax import lax
from jax.experimental import pallas as pl
from jax.experimental.pallas import tpu as pltpu

F32 = jnp.float32
BF16 = jnp.bfloat16

D_MODEL = 1024
MIX = 512
NORM_EPS = 1e-6
PAST_LEN = 16384
S5_GROUPS = 32
S5_GROUP = 16
S5_STATE = 64
S5_CHUNK = 16
S5_PAIRS = S5_GROUPS // 2
HEADS = 4
DK = 64
DV = 128
QK = HEADS * DK
ROPE_BASE = 10000.0
RET_CHUNK = 256
GLA_CHUNK = 64
GLA_BLOCK = 256
GLA_RANK = 16
GLA_TAU = 16.0
LRU_BLOCKS = 8
LRU_C = 8.0
CONV_W = 4
PEER_HEADS = 8
PEER_NK = 128
PEER_EXPERTS = PEER_NK * PEER_NK
PEER_TOPK = 16
PEER_EBLK = 1024
PEER_ROWS = PEER_EBLK // PEER_NK

COL_MG = 0
COL_U = 4096
COL_RQ = 4608
COL_RK = 4864
COL_RV = 5120
COL_RG = 5632
COL_LX = 6144
COL_LG = 6656
COL_GQ = 7168
COL_GK = 7424
COL_GV = 7680
COL_GG = 8192
IN_MAIN = 8704
IN_TILE = 512
INPROJ_ROWS = 1024

VMEM_LIMIT = 52 * 1024 * 1024


def _cparams(sem):
    return pltpu.CompilerParams(dimension_semantics=sem, vmem_limit_bytes=VMEM_LIMIT)


def _dot(a, b):
    return jnp.dot(a, b, preferred_element_type=F32)


def _dot_nt(a, b):
    return lax.dot_general(a, b, (((1,), (1,)), ((), ())), preferred_element_type=F32)


def _dot_tn(a, b):
    return lax.dot_general(a, b, (((0,), (0,)), ((), ())), preferred_element_type=F32)


def _sigmoid(x):
    return 1.0 / (1.0 + jnp.exp(-x))


def _silu(x):
    return x * _sigmoid(x)


def _gelu(x):
    return 0.5 * x * (1.0 + jnp.tanh(math.sqrt(2.0 / math.pi) * (x + 0.044715 * (x * x * x))))


def _softplus(x):
    return jnp.maximum(x, 0.0) + jnp.log1p(jnp.exp(-jnp.abs(x)))


def _rms(x, g):
    return x * lax.rsqrt(jnp.mean(x * x, axis=-1, keepdims=True) + NORM_EPS) * g


def _inproj_kernel(x_ref, g_ref, w_ref, wga_ref, o_ref, ga_ref, n_ref):
    @pl.when(pl.program_id(1) == 0)
    def _():
        n = _rms(x_ref[...], g_ref[...]).astype(BF16)
        n_ref[...] = n
        ga_ref[...] = _dot(n, wga_ref[...])

    o_ref[...] = _dot(n_ref[...], w_ref[...])


def _inproj(x, g, w, wga, tm):
    t = x.shape[0]
    return pl.pallas_call(
        _inproj_kernel,
        out_shape=(jax.ShapeDtypeStruct((t, IN_MAIN), F32), jax.ShapeDtypeStruct((t, 128), F32)),
        grid=(t // tm, IN_MAIN // IN_TILE),
        in_specs=[pl.BlockSpec((tm, D_MODEL), lambda i, j: (i, 0)),
                  pl.BlockSpec((1, D_MODEL), lambda i, j: (0, 0)),
                  pl.BlockSpec((D_MODEL, IN_TILE), lambda i, j: (0, j)),
                  pl.BlockSpec((D_MODEL, 128), lambda i, j: (0, 0))],
        out_specs=(pl.BlockSpec((tm, IN_TILE), lambda i, j: (i, j)),
                   pl.BlockSpec((tm, 128), lambda i, j: (i, 0))),
        scratch_shapes=[pltpu.VMEM((tm, D_MODEL), BF16)],
        compiler_params=_cparams(("parallel", "arbitrary")),
        name="inproj",
    )(x, g, w, wga)


def _prep_kernel(rq_ref, rk_ref, ga_ref, cos_ref, sin_ref, wa2_ref, ba2_ref, qr_ref, kr_ref, la_ref):
    cos = cos_ref[...]
    sin = sin_ref[...]
    q = rq_ref[...]
    k = rk_ref[...]
    q1, q2 = q[:, :128], q[:, 128:]
    k1, k2 = k[:, :128], k[:, 128:]
    qr_ref[:, :128] = q1 * cos - q2 * sin
    qr_ref[:, 128:] = q1 * sin + q2 * cos
    kr_ref[:, :128] = (k1 * cos - k2 * sin) * (DK ** -0.5)
    kr_ref[:, 128:] = (k1 * sin + k2 * cos) * (DK ** -0.5)
    z = _dot(ga_ref[...].astype(BF16), wa2_ref[...]) + ba2_ref[...]
    la_ref[...] = (jnp.minimum(z, 0.0) - jnp.log1p(jnp.exp(-jnp.abs(z)))) * (1.0 / GLA_TAU)


def _prep(p, ga, cos, sin, wa2, ba2, tm):
    t = p.shape[0]
    n_pos = cos.shape[0] // tm
    return pl.pallas_call(
        _prep_kernel,
        out_shape=(jax.ShapeDtypeStruct((t, QK), F32),) * 3,
        grid=(t // tm,),
        in_specs=[pl.BlockSpec((tm, QK), lambda i: (i, COL_RQ // QK)),
                  pl.BlockSpec((tm, QK), lambda i: (i, COL_RK // QK)),
                  pl.BlockSpec((tm, 128), lambda i: (i, 0)),
                  pl.BlockSpec((tm, 128), lambda i: (i % n_pos, 0)),
                  pl.BlockSpec((tm, 128), lambda i: (i % n_pos, 0)),
                  pl.BlockSpec((128, QK), lambda i: (0, 0)),
                  pl.BlockSpec((1, QK), lambda i: (0, 0))],
        out_specs=(pl.BlockSpec((tm, QK), lambda i: (i, 0)),) * 3,
        compiler_params=_cparams(("parallel",)),
        name="prep",
    )(p, p, ga, cos, sin, wa2, ba2)


def _s5_prompt_kernel(u_ref, m_ref, wzr_ref, wzi_ref, wyr_ref, wyi_ref, ar_ref, ai_ref,
                      y_ref, sr_ref, si_ref, zr_ref, zi_ref, *, n_chunks, bsz):
    u = u_ref[0].astype(BF16)
    zr_ref[...] = _dot(u, wzr_ref[0])
    zi_ref[...] = _dot(u, wzi_ref[0])
    a_re = jnp.broadcast_to(ar_ref[0], (bsz, 128))
    a_im = jnp.broadcast_to(ai_ref[0], (bsz, 128))

    def body(n, carry):
        s_re, s_im = carry
        rows = pl.ds(pl.multiple_of(n * bsz, bsz), bsz)
        z_re = zr_ref[rows, :]
        z_im = zi_ref[rows, :]
        zr_ref[rows, :] = s_re
        zi_ref[rows, :] = s_im
        return (a_re * s_re - a_im * s_im + z_re, a_re * s_im + a_im * s_re + z_im)

    zero = jnp.zeros((bsz, 128), F32)
    s_re, s_im = lax.fori_loop(0, n_chunks, body, (zero, zero))
    sr_ref[0] = s_re
    si_ref[0] = s_im
    y_ref[0] = (_dot(u, m_ref[0]) + _dot(zr_ref[...].astype(BF16), wyr_ref[0])
                + _dot(zi_ref[...].astype(BF16), wyi_ref[0]))


def _s5_prompt(u, w, bsz, seq):
    n_chunks = seq // S5_CHUNK
    rows = n_chunks * bsz
    ug = (u.reshape(bsz, n_chunks, S5_CHUNK, S5_PAIRS, 2, S5_GROUP)
          .transpose(3, 1, 0, 4, 2, 5).reshape(S5_PAIRS, rows, 512))
    spec3 = lambda a, b: pl.BlockSpec((1, a, b), lambda j: (j, 0, 0))
    y, s_re, s_im = pl.pallas_call(
        functools.partial(_s5_prompt_kernel, n_chunks=n_chunks, bsz=bsz),
        out_shape=(jax.ShapeDtypeStruct((S5_PAIRS, rows, 512), F32),
                   jax.ShapeDtypeStruct((S5_PAIRS, bsz, 128), F32),
                   jax.ShapeDtypeStruct((S5_PAIRS, bsz, 128), F32)),
        grid=(S5_PAIRS,),
        in_specs=[spec3(rows, 512), spec3(512, 512), spec3(512, 128), spec3(512, 128),
                  spec3(128, 512), spec3(128, 512), spec3(1, 128), spec3(1, 128)],
        out_specs=(spec3(rows, 512), spec3(bsz, 128), spec3(bsz, 128)),
        scratch_shapes=[pltpu.VMEM((rows, 128), F32), pltpu.VMEM((rows, 128), F32)],
        compiler_params=_cparams(("parallel",)),
        name="s5_prompt",
    )(ug, w["m"], w["wz_re"], w["wz_im"], w["wy_re"], w["wy_im"], w["a16_re"], w["a16_im"])
    y = (y.reshape(S5_PAIRS, n_chunks, bsz, 2, S5_CHUNK, S5_GROUP)
         .transpose(2, 1, 4, 0, 3, 5).reshape(bsz * seq, MIX))
    to_state = lambda s: s.reshape(S5_PAIRS, bsz, 2, S5_STATE).transpose(1, 0, 2, 3).reshape(bsz, S5_GROUPS, S5_STATE)
    return y, to_state(s_re), to_state(s_im)


def _s5_step_kernel(u_ref, hr_ref, hi_ref, bre_ref, bim_ref, abr_ref, abi_ref, cre_ref, cim_ref,
                    y_ref, sr_ref, si_ref):
    u = u_ref[...].astype(BF16)
    h_re = hr_ref[...]
    h_im = hi_ref[...]
    ab_re = abr_ref[...]
    ab_im = abi_ref[...]
    s_re = _dot(u, bre_ref[...]) + (ab_re * h_re - ab_im * h_im)
    s_im = _dot(u, bim_ref[...]) + (ab_re * h_im + ab_im * h_re)
    sr_ref[...] = s_re
    si_ref[...] = s_im
    y_ref[...] = _dot(s_re.astype(BF16), cre_ref[...]) - _dot(s_im.astype(BF16), cim_ref[...])


def _s5_step(p, h_re, h_im, w):
    t = p.shape[0]
    ns = S5_GROUPS * S5_STATE
    full = lambda a, b: pl.BlockSpec((a, b), lambda i: (0, 0))
    return pl.pallas_call(
        _s5_step_kernel,
        out_shape=(jax.ShapeDtypeStruct((t, MIX), F32), jax.ShapeDtypeStruct((t, ns), F32),
                   jax.ShapeDtypeStruct((t, ns), F32)),
        grid=(1,),
        in_specs=[pl.BlockSpec((t, MIX), lambda i: (0, COL_U // MIX)), full(t, ns), full(t, ns),
                  full(MIX, ns), full(MIX, ns), full(1, ns), full(1, ns), full(ns, MIX), full(ns, MIX)],
        out_specs=(full(t, MIX), full(t, ns), full(t, ns)),
        compiler_params=_cparams(("arbitrary",)),
        name="s5_step",
    )(p, h_re, h_im, w["bd_re"], w["bd_im"], w["ab_re"], w["ab_im"], w["cd_re"], w["cd_im"])


def _lru_kernel(x_ref, g_ref, h0_ref, c0_ref, cw_ref, cb_ref, wa_ref, ba_ref, wx_ref, bx_ref, lam_ref,
                y_ref, hf_ref, cf_ref, xbuf, hs, abuf, bbuf, *, steps, bsz):
    rows = steps * bsz
    halo = (CONV_W - 1) * bsz

    @pl.when(pl.program_id(0) == 0)
    def _():
        xbuf[0:halo, :] = c0_ref[...]
        hs[...] = h0_ref[...]

    xbuf[halo:halo + rows, :] = x_ref[...]
    xc = cb_ref[...] + xbuf[0:rows, :] * cw_ref[0:1, :]
    for j in range(1, CONV_W):
        xc = xc + xbuf[j * bsz:j * bsz + rows, :] * cw_ref[j:j + 1, :]
    xcb = xc.astype(BF16)
    r = _sigmoid(_dot(xcb, wa_ref[...]) + ba_ref[...])
    i = _sigmoid(_dot(xcb, wx_ref[...]) + bx_ref[...])
    log_a = (-LRU_C) * r * _softplus(-lam_ref[...])
    a = jnp.exp(log_a)
    abuf[...] = a
    bbuf[...] = jnp.sqrt(-jnp.tanh(log_a) * (a * a + 1.0)) * (i * xc)

    def body(t, h):
        sl = pl.ds(pl.multiple_of(t * bsz, bsz), bsz)
        h = abuf[sl, :] * h + bbuf[sl, :]
        bbuf[sl, :] = h
        return h

    h = lax.fori_loop(0, steps, body, hs[...])
    hs[...] = h
    hf_ref[...] = h
    y_ref[...] = bbuf[...] * _gelu(g_ref[...])
    tail = xbuf[rows:rows + halo, :]
    xbuf[0:halo, :] = tail
    cf_ref[...] = tail


def _lru(x, g, h0, c0, w, bsz, steps):
    t = x.shape[0]
    rows = steps * bsz
    halo = (CONV_W - 1) * bsz
    full = lambda a, b: pl.BlockSpec((a, b), lambda i: (0, 0))
    return pl.pallas_call(
        functools.partial(_lru_kernel, steps=steps, bsz=bsz),
        out_shape=(jax.ShapeDtypeStruct((t, MIX), F32), jax.ShapeDtypeStruct((bsz, MIX), F32),
                   jax.ShapeDtypeStruct((halo, MIX), F32)),
        grid=(t // rows,),
        in_specs=[pl.BlockSpec((rows, MIX), lambda i: (i, 0)), pl.BlockSpec((rows, MIX), lambda i: (i, 0)),
                  full(bsz, MIX), full(halo, MIX), full(CONV_W, MIX), full(1, MIX),
                  full(MIX, MIX), full(1, MIX), full(MIX, MIX), full(1, MIX), full(1, MIX)],
        out_specs=(pl.BlockSpec((rows, MIX), lambda i: (i, 0)), full(bsz, MIX), full(halo, MIX)),
        scratch_shapes=[pltpu.VMEM((rows + halo, MIX), F32), pltpu.VMEM((bsz, MIX), F32),
                        pltpu.VMEM((rows, MIX), F32), pltpu.VMEM((rows, MIX), F32)],
        compiler_params=_cparams(("arbitrary",)),
        name="rglru",
    )(x, g, h0, c0, w["conv_w"], w["conv_b"], w["wa"], w["ba"], w["wx"], w["bx"], w["lam"])


def _ret_prompt_kernel(q_ref, k_ref, v_ref, dec_ref, wq_ref, wk_ref, gc_ref, hm_ref, o_ref, sf_ref, s_ref):
    @pl.when(pl.program_id(1) == 0)
    def _():
        s_ref[...] = jnp.zeros_like(s_ref)

    q = q_ref[...]
    k = k_ref[...]
    kb = k.astype(BF16)
    vb = v_ref[...].astype(BF16)
    sb = s_ref[...].astype(BF16)
    qw = q * wq_ref[...]
    for h in range(HEADS):
        hm = hm_ref[h]
        vs = slice(h * DV, (h + 1) * DV)
        sc = _dot_nt((q * hm).astype(BF16), kb) * dec_ref[h]
        o_ref[:, vs] = _dot(sc.astype(BF16), vb[:, vs]) + _dot((qw * hm).astype(BF16), sb[:, vs])
    s_new = s_ref[...] * gc_ref[...] + _dot_tn((k * wk_ref[...]).astype(BF16), vb)
    s_ref[...] = s_new
    sf_ref[0] = s_new


def _ret_prompt(qr, kr, p, c, bsz, seq):
    ch = RET_CHUNK
    n = seq // ch
    full2 = lambda a, b: pl.BlockSpec((a, b), lambda bi, ci: (0, 0))
    o, sf = pl.pallas_call(
        _ret_prompt_kernel,
        out_shape=(jax.ShapeDtypeStruct((bsz * seq, MIX), F32), jax.ShapeDtypeStruct((bsz, QK, MIX), F32)),
        grid=(bsz, n),
        in_specs=[pl.BlockSpec((ch, QK), lambda bi, ci: (bi * n + ci, 0)),
                  pl.BlockSpec((ch, QK), lambda bi, ci: (bi * n + ci, 0)),
                  pl.BlockSpec((ch, MIX), lambda bi, ci: (bi * n + ci, COL_RV // MIX)),
                  pl.BlockSpec((HEADS, ch, ch), lambda bi, ci: (0, 0, 0)),
                  full2(ch, QK), full2(ch, QK), full2(1, MIX),
                  pl.BlockSpec((HEADS, 1, QK), lambda bi, ci: (0, 0, 0))],
        out_specs=(pl.BlockSpec((ch, MIX), lambda bi, ci: (bi * n + ci, 0)),
                   pl.BlockSpec((1, QK, MIX), lambda bi, ci: (bi, 0, 0))),
        scratch_shapes=[pltpu.VMEM((QK, MIX), F32)],
        compiler_params=_cparams(("parallel", "arbitrary")),
        name="ret_prompt",
    )(qr, kr, p, c["dec"], c["wq"], c["wk"], c["gc"], c["hmask"])
    s5d = sf.reshape(bsz, 2, HEADS, 32, HEADS, DV)
    st = jnp.stack([s5d[:, :, h, :, h, :] for h in range(HEADS)], axis=1)
    return o, st.reshape(bsz, HEADS, DK, DV)


def _gla_prompt_kernel(q_ref, k_ref, v_ref, la_ref, tril_ref, hm_ref, o_ref, sf_ref, s_ref):
    @pl.when(pl.program_id(1) == 0)
    def _():
        s_ref[...] = jnp.zeros_like(s_ref)

    tril = tril_ref[...]
    for c in range(GLA_BLOCK // GLA_CHUNK):
        rs = slice(c * GLA_CHUNK, (c + 1) * GLA_CHUNK)
        b = jnp.dot(tril, la_ref[rs, :], preferred_element_type=F32, precision=lax.Precision.HIGHEST)
        b_last = b[GLA_CHUNK - 1:GLA_CHUNK, :]
        q = q_ref[rs, :] * (DK ** -0.5)
        k = k_ref[rs, :]
        q_t = q * jnp.exp(b)
        k_t = (k * jnp.exp(-b)).astype(BF16)
        k_s = (k * jnp.exp(b_last - b)).astype(BF16)
        vb = v_ref[rs, :].astype(BF16)
        sb = s_ref[...].astype(BF16)
        for h in range(HEADS):
            qm = (q_t * hm_ref[h]).astype(BF16)
            vs = slice(h * DV, (h + 1) * DV)
            sc = jnp.where(tril > 0.0, _dot_nt(qm, k_t), 0.0)
            o_ref[rs, vs] = _dot(sc.astype(BF16), vb[:, vs]) + _dot_nt(qm, sb[vs, :])
        s_ref[...] = s_ref[...] * jnp.exp(b_last) + _dot_tn(vb, k_s)
    sf_ref[0] = s_ref[...]


def _gla_prompt(p, la, c, bsz, seq):
    blk = GLA_BLOCK
    n = seq // blk
    o, sf = pl.pallas_call(
        _gla_prompt_kernel,
        out_shape=(jax.ShapeDtypeStruct((bsz * seq, MIX), F32), jax.ShapeDtypeStruct((bsz, MIX, QK), F32)),
        grid=(bsz, n),
        in_specs=[pl.BlockSpec((blk, QK), lambda bi, ci: (bi * n + ci, COL_GQ // QK)),
                  pl.BlockSpec((blk, QK), lambda bi, ci: (bi * n + ci, COL_GK // QK)),
                  pl.BlockSpec((blk, MIX), lambda bi, ci: (bi * n + ci, COL_GV // MIX)),
                  pl.BlockSpec((blk, QK), lambda bi, ci: (bi * n + ci, 0)),
                  pl.BlockSpec((GLA_CHUNK, GLA_CHUNK), lambda bi, ci: (0, 0)),
                  pl.BlockSpec((HEADS, 1, QK), lambda bi, ci: (0, 0, 0))],
        out_specs=(pl.BlockSpec((blk, MIX), lambda bi, ci: (bi * n + ci, 0)),
                   pl.BlockSpec((1, MIX, QK), lambda bi, ci: (bi, 0, 0))),
        scratch_shapes=[pltpu.VMEM((MIX, QK), F32)],
        compiler_params=_cparams(("parallel", "arbitrary")),
        name="gla_prompt",
    )(p, p, p, la, c["tril"], c["hmask"])
    s5d = sf.reshape(bsz, HEADS, DV, HEADS, DK)
    st = jnp.stack([s5d[:, h, :, h, :] for h in range(HEADS)], axis=1)
    return o, st.transpose(0, 1, 3, 2)


STEP_GROUP = 8


def _mat_step_kernel(rq_ref, rk_ref, rv_ref, rs_ref, gq_ref, gk_ref, ga_ref, gv_ref, gs_ref, rdec_ref,
                     ro_ref, rso_ref, go_ref, gso_ref):
    rq = rq_ref[0]
    rk = rk_ref[0]
    gq = gq_ref[0] * (DK ** -0.5)
    gk = gk_ref[0]
    ga = jnp.exp(ga_ref[0])
    for i in range(STEP_GROUP):
        for h in range(HEADS):
            ks = slice(h * DK, (h + 1) * DK)
            vs = slice(h * DV, (h + 1) * DV)
            s1 = rs_ref[i, h] * rdec_ref[h] + rk[ks, i:i + 1] * rv_ref[i:i + 1, vs]
            rso_ref[i, h] = s1
            ro_ref[i:i + 1, vs] = jnp.sum(rq[ks, i:i + 1] * s1, axis=0, keepdims=True)
            s1 = gs_ref[i, h] * ga[ks, i:i + 1] + gk[ks, i:i + 1] * gv_ref[i:i + 1, vs]
            gso_ref[i, h] = s1
            go_ref[i:i + 1, vs] = jnp.sum(gq[ks, i:i + 1] * s1, axis=0, keepdims=True)


def _mat_step(qr, kr, la, p, s_ret, s_gla, rdec):
    t = p.shape[0]
    g = STEP_GROUP
    ng = t // g

    def tr(x):
        return x.reshape(ng, g, QK).transpose(0, 2, 1)

    def unsplit(x):
        return x.reshape(t, 2, HEADS, 32).transpose(0, 2, 1, 3).reshape(t, QK)

    gq = p[:, COL_GQ:COL_GQ + QK]
    gk = p[:, COL_GK:COL_GK + QK]
    tspec = pl.BlockSpec((1, QK, g), lambda i: (i, 0, 0))
    sspec = pl.BlockSpec((g, HEADS, DK, DV), lambda i: (i, 0, 0, 0))
    vspec = lambda col: pl.BlockSpec((g, MIX), lambda i: (i, col // MIX))
    ospec = pl.BlockSpec((g, MIX), lambda i: (i, 0))
    return pl.pallas_call(
        _mat_step_kernel,
        out_shape=(jax.ShapeDtypeStruct((t, MIX), F32), jax.ShapeDtypeStruct(s_ret.shape, F32),
                   jax.ShapeDtypeStruct((t, MIX), F32), jax.ShapeDtypeStruct(s_gla.shape, F32)),
        grid=(ng,),
        in_specs=[tspec, tspec, vspec(COL_RV), sspec, tspec, tspec, tspec, vspec(COL_GV), sspec,
                  pl.BlockSpec((HEADS, 1, DV), lambda i: (0, 0, 0))],
        out_specs=(ospec, sspec, ospec, sspec),
        compiler_params=_cparams(("parallel",)),
        name="mat_step",
    )(tr(unsplit(qr)), tr(unsplit(kr)), p, s_ret, tr(gq), tr(gk), tr(la), p, s_gla, rdec)


def _head_norm(o, g):
    parts = []
    for h in range(HEADS):
        oh = o[:, h * DV:(h + 1) * DV]
        mu = jnp.mean(oh, axis=-1, keepdims=True)
        d = oh - mu
        var = jnp.mean(d * d, axis=-1, keepdims=True)
        parts.append(d * lax.rsqrt(var + NORM_EPS))
    return jnp.concatenate(parts, axis=-1) * g


def _merge_kernel(x_ref, mg_ref, u_ref, rg_ref, gg_ref, ys_ref, ro_ref, yc_ref, go_ref,
                  d_ref, wglu_ref, bglu_ref, gret_ref, ggla_ref, wb_ref, wout_ref, gffn_ref,
                  h_ref, n_ref):
    ya = _gelu(ys_ref[...] + d_ref[...] * u_ref[...])
    ya = ya * _sigmoid(_dot(ya.astype(BF16), wglu_ref[...]) + bglu_ref[...])
    yb = _silu(rg_ref[...]) * _head_norm(ro_ref[...], gret_ref[...])
    yc = yc_ref[...]
    yd = _silu(gg_ref[...]) * _head_norm(go_ref[...], ggla_ref[...])
    m = None
    for i, y in enumerate((ya, yb, yc, yd)):
        z = _dot(y.astype(BF16), wb_ref[i])
        gz = _sigmoid(mg_ref[:, i * D_MODEL:(i + 1) * D_MODEL]) * z
        m = gz if m is None else m + gz
    h = x_ref[...] + _dot(m.astype(BF16), wout_ref[...])
    h_ref[...] = h
    n_ref[...] = _rms(h, gffn_ref[...]).astype(BF16)


def _merge(x, p, ys, ro, yc, go, w, tm):
    t = x.shape[0]
    row = lambda width, col=0: pl.BlockSpec((tm, width), lambda i: (i, col // width))
    full = lambda a, b: pl.BlockSpec((a, b), lambda i: (0, 0))
    return pl.pallas_call(
        _merge_kernel,
        out_shape=(jax.ShapeDtypeStruct((t, D_MODEL), F32), jax.ShapeDtypeStruct((t, D_MODEL), BF16)),
        grid=(t // tm,),
        in_specs=[row(D_MODEL), row(4 * D_MODEL, COL_MG), row(MIX, COL_U), row(MIX, COL_RG), row(MIX, COL_GG),
                  row(MIX), row(MIX), row(MIX), row(MIX),
                  full(1, MIX), full(MIX, MIX), full(1, MIX), full(1, MIX), full(1, MIX),
                  pl.BlockSpec((4, MIX, D_MODEL), lambda i: (0, 0, 0)), full(D_MODEL, D_MODEL), full(1, D_MODEL)],
        out_specs=(row(D_MODEL), row(D_MODEL)),
        compiler_params=_cparams(("parallel",)),
        name="merge",
    )(x, p, p, p, p, ys, ro, yc, go, w["s5_d"], w["w_glu"], w["b_glu"], w["ret_norm"], w["gla_norm"],
      w["w_branch"], w["w_out"], w["norm_ffn"])


def _scores_kernel(n_ref, wq_ref, sk_ref, st_ref):
    q = _dot(n_ref[...], wq_ref[...]).astype(BF16)
    for hp in range(2 * PEER_HEADS):
        st_ref[hp] = _dot_nt(sk_ref[hp], q[:, hp * PEER_NK:(hp + 1) * PEER_NK])


def _scores(n2, wq, sk, tm):
    t = n2.shape[0]
    return pl.pallas_call(
        _scores_kernel,
        out_shape=jax.ShapeDtypeStruct((2 * PEER_HEADS, PEER_NK, t), F32),
        grid=(t // tm,),
        in_specs=[pl.BlockSpec((tm, D_MODEL), lambda i: (i, 0)),
                  pl.BlockSpec((D_MODEL, 2 * PEER_HEADS * PEER_NK), lambda i: (0, 0)),
                  pl.BlockSpec((2 * PEER_HEADS, PEER_NK, PEER_NK), lambda i: (0, 0, 0))],
        out_specs=pl.BlockSpec((2 * PEER_HEADS, PEER_NK, tm), lambda i: (0, 0, i)),
        compiler_params=_cparams(("parallel",)),
        name="peer_scores",
    )(n2, wq, sk)


PEER_NTOP = PEER_TOPK + 1
PEER_CAND_SPANS = tuple(PEER_NTOP // (a + 1) for a in range(PEER_NTOP))
PEER_NCAND = sum(PEER_CAND_SPANS)
PEER_CAND_ROWS = -(-PEER_NCAND // 8) * 8


def _top_desc(cur, n, dst_ref=None):
    vals = []
    for r in range(n):
        m = jnp.max(cur, axis=0, keepdims=True)
        vals.append(m)
        if dst_ref is not None:
            dst_ref[r:r + 1, :] = m
        if r + 1 < n:
            cur = jnp.where(cur >= m, -jnp.inf, cur)
    return vals


def _peer_prologue(st_ref, e0_ref, e1_ref, tau_ref, a1_ref, cand_ref):
    cand_ref[...] = jnp.full(cand_ref.shape, -jnp.inf, F32)
    for h in range(PEER_HEADS):
        s0 = st_ref[2 * h]
        s1 = st_ref[2 * h + 1]
        a0 = _top_desc(s0, PEER_NTOP)
        a1 = _top_desc(s1, PEER_NTOP, a1_ref)
        off = 0
        for a, span in enumerate(PEER_CAND_SPANS):
            cand_ref[off:off + span, :] = a0[a] + a1_ref[0:span, :]
            off += span
        cand = cand_ref[...]
        top = _top_desc(cand, PEER_NTOP)
        tau = 0.5 * (top[PEER_TOPK - 1] + top[PEER_TOPK])
        z = jnp.sum(jnp.where(cand >= tau, jnp.exp(cand - top[0]), 0.0), axis=0, keepdims=True)
        e0_ref[h] = jnp.exp(s0 - a0[0])
        e1_ref[h] = jnp.exp(s1 - a1[0]) / z
        tau_ref[h] = tau


def _peer_kernel(st_ref, n_ref, h_ref, u_ref, vp_ref, vl_ref, gf_ref, o_ref,
                 e0_ref, e1_ref, tau_ref, a1_ref, cand_ref, hid_ref, a_ref, acc_ref, *, final_norm):
    j = pl.program_id(1)

    @pl.when(j == 0)
    def _():
        acc_ref[...] = jnp.zeros_like(acc_ref)
        a_ref[...] = jnp.zeros_like(a_ref)
        _peer_prologue(st_ref, e0_ref, e1_ref, tau_ref, a1_ref, cand_ref)

    cur = j % 2
    hid_ref[...] = _dot_nt(u_ref[...], n_ref[...])
    acc_ref[...] += _dot_tn(a_ref[1 - cur], vp_ref[...])
    for r in range(PEER_ROWS):
        i0 = j * PEER_ROWS + r
        g = None
        for h in range(PEER_HEADS):
            theta = tau_ref[h] - st_ref[2 * h, pl.ds(i0, 1), :]
            w = jnp.where(st_ref[2 * h + 1] >= theta, e1_ref[h], 0.0) * e0_ref[h, pl.ds(i0, 1), :]
            g = w if g is None else g + w
        rows = slice(r * PEER_NK, (r + 1) * PEER_NK)
        a_ref[cur, rows, :] = (g * _gelu(hid_ref[rows, :])).astype(BF16)

    @pl.when(j == pl.num_programs(1) - 1)
    def _():
        out = h_ref[...] + acc_ref[...] + _dot_tn(a_ref[cur], vl_ref[...])
        if final_norm:
            out = _rms(out, gf_ref[...])
        o_ref[...] = out


def _peer(st, n2, h, u, v, gfin, tm, final_norm):
    t = n2.shape[0]
    nj = PEER_EXPERTS // PEER_EBLK
    return pl.pallas_call(
        functools.partial(_peer_kernel, final_norm=final_norm),
        out_shape=jax.ShapeDtypeStruct((t, D_MODEL), F32),
        grid=(t // tm, nj),
        in_specs=[pl.BlockSpec((2 * PEER_HEADS, PEER_NK, tm), lambda i, j: (0, 0, i)),
                  pl.BlockSpec((tm, D_MODEL), lambda i, j: (i, 0)),
                  pl.BlockSpec((tm, D_MODEL), lambda i, j: (i, 0)),
                  pl.BlockSpec((PEER_EBLK, D_MODEL), lambda i, j: (j, 0)),
                  pl.BlockSpec((PEER_EBLK, D_MODEL), lambda i, j: (jnp.maximum(j - 1, 0), 0)),
                  pl.BlockSpec((PEER_EBLK, D_MODEL), lambda i, j: (nj - 1, 0)),
                  pl.BlockSpec((1, D_MODEL), lambda i, j: (0, 0))],
        out_specs=pl.BlockSpec((tm, D_MODEL), lambda i, j: (i, 0)),
        scratch_shapes=[pltpu.VMEM((PEER_HEADS, PEER_NK, tm), F32), pltpu.VMEM((PEER_HEADS, PEER_NK, tm), F32),
                        pltpu.VMEM((PEER_HEADS, 1, tm), F32),
                        pltpu.VMEM((PEER_NTOP + 7, tm), F32), pltpu.VMEM((PEER_CAND_ROWS, tm), F32),
                        pltpu.VMEM((PEER_EBLK, tm), F32), pltpu.VMEM((2, PEER_EBLK, tm), BF16),
                        pltpu.VMEM((tm, D_MODEL), F32)],
        compiler_params=_cparams(("parallel", "arbitrary")),
        name="peer",
    )(st, n2, h, u, v, v, gfin)


def _blockdiag(x, n):
    nm, r, c = x.shape
    m = nm // n
    return jnp.einsum("mgrc,gh->mgrhc", x.reshape(m, n, r, c), jnp.eye(n, dtype=x.dtype)).reshape(m, n * r, n * c)


def _s5_weights(lam_re, lam_im, log_dt, b_re, b_im, c_re, c_im):
    hi = lax.Precision.HIGHEST
    dt = jnp.exp(log_dt)[:, None]
    mag = jnp.exp(lam_re * dt)
    ab_re = mag * jnp.cos(lam_im * dt)
    ab_im = mag * jnp.sin(lam_im * dt)
    den = lam_re * lam_re + lam_im * lam_im
    zr = ((ab_re - 1.0) * lam_re + ab_im * lam_im) / den
    zi = (ab_im * lam_re - (ab_re - 1.0) * lam_im) / den
    bb_re = zr[..., None] * b_re - zi[..., None] * b_im
    bb_im = zr[..., None] * b_im + zi[..., None] * b_re
    kk = jnp.arange(S5_CHUNK + 1, dtype=F32)[:, None, None]
    pmag = jnp.exp(lam_re * dt * kk)
    pw_re = pmag * jnp.cos(lam_im * dt * kk)
    pw_im = pmag * jnp.sin(lam_im * dt * kk)
    cp_re = c_re[None] * pw_re[:, :, None, :] - c_im[None] * pw_im[:, :, None, :]
    cp_im = c_re[None] * pw_im[:, :, None, :] + c_im[None] * pw_re[:, :, None, :]
    kern = (jnp.einsum("tgop,gpc->tgoc", cp_re[:S5_CHUNK], bb_re, precision=hi)
            - jnp.einsum("tgop,gpc->tgoc", cp_im[:S5_CHUNK], bb_im, precision=hi))
    step = jnp.arange(S5_CHUNK)
    lag = step[None, :] - step[:, None]
    m = jnp.where((lag >= 0)[:, :, None, None, None], kern[jnp.clip(lag, 0)], 0.0)
    m = m.transpose(2, 0, 4, 1, 3).reshape(S5_GROUPS, 256, 256)
    rev = pw_re[S5_CHUNK - 1 - step], pw_im[S5_CHUNK - 1 - step]
    wz_re = rev[0][..., None] * bb_re[None] - rev[1][..., None] * bb_im[None]
    wz_im = rev[0][..., None] * bb_im[None] + rev[1][..., None] * bb_re[None]
    to_z = lambda w: w.transpose(1, 0, 3, 2).reshape(S5_GROUPS, 256, S5_STATE)
    wy_re = cp_re[1:].transpose(1, 3, 0, 2).reshape(S5_GROUPS, S5_STATE, 256)
    wy_im = (-cp_im[1:]).transpose(1, 3, 0, 2).reshape(S5_GROUPS, S5_STATE, 256)
    eye = jnp.eye(S5_GROUPS, dtype=F32)
    return {
        "m": _blockdiag(m, 2).astype(BF16),
        "wz_re": _blockdiag(to_z(wz_re), 2).astype(BF16), "wz_im": _blockdiag(to_z(wz_im), 2).astype(BF16),
        "wy_re": _blockdiag(wy_re, 2).astype(BF16), "wy_im": _blockdiag(wy_im, 2).astype(BF16),
        "a16_re": pw_re[S5_CHUNK].reshape(S5_PAIRS, 1, 128), "a16_im": pw_im[S5_CHUNK].reshape(S5_PAIRS, 1, 128),
        "ab_re": ab_re.reshape(1, -1), "ab_im": ab_im.reshape(1, -1),
        "bd_re": jnp.einsum("gpc,gh->gchp", bb_re, eye).reshape(MIX, -1).astype(BF16),
        "bd_im": jnp.einsum("gpc,gh->gchp", bb_im, eye).reshape(MIX, -1).astype(BF16),
        "cd_re": jnp.einsum("gcp,gh->gphc", c_re, eye).reshape(-1, MIX).astype(BF16),
        "cd_im": jnp.einsum("gcp,gh->gphc", c_im, eye).reshape(-1, MIX).astype(BF16),
    }


def _ret_consts():
    ch = RET_CHUNK
    log_g = jnp.log1p(-jnp.exp2(-5.0 - jnp.arange(HEADS, dtype=F32)))
    idx = jnp.arange(ch, dtype=F32)
    diff = idx[:, None] - idx[None, :]
    mask = diff >= 0
    dec = jnp.where(mask[None], jnp.exp(log_g[:, None, None] * jnp.where(mask, diff, 0.0)[None]), 0.0)
    lane_head = (jnp.arange(QK) % 128) // 32
    lg_lane = log_g[lane_head]
    wq = jnp.exp(lg_lane[None, :] * (idx + 1.0)[:, None])
    wk = jnp.exp(lg_lane[None, :] * (ch - 1.0 - idx)[:, None])
    gc = jnp.repeat(jnp.exp(log_g * ch), DV)[None, :]
    hmask = (lane_head[None, :] == jnp.arange(HEADS)[:, None]).astype(F32)[:, None, :]
    rdec = jnp.broadcast_to(jnp.exp(log_g)[:, None, None], (HEADS, 1, DV))
    return {"dec": dec, "wq": wq, "wk": wk, "gc": gc, "hmask": hmask, "rdec": rdec}


def _gla_consts():
    tril = jnp.tril(jnp.ones((GLA_CHUNK, GLA_CHUNK), F32))
    hmask = ((jnp.arange(QK) // DK)[None, :] == jnp.arange(HEADS)[:, None]).astype(F32)[:, None, :]
    return {"tril": tril, "hmask": hmask}


def _rope_tables(pos):
    half = DK // 2
    inv = ROPE_BASE ** (-jnp.arange(half, dtype=F32) / half)
    ang = pos.astype(F32)[:, None] * inv[None, :]
    return jnp.tile(jnp.cos(ang), (1, HEADS)), jnp.tile(jnp.sin(ang), (1, HEADS))


def _layer_weights(lp):
    w_in = lp["w_in"]
    cuts = {}
    off = 0
    for name, size in (("u", 512), ("rq", 256), ("rk", 256), ("rv", 512), ("rg", 512), ("lx", 512), ("lg", 512),
                       ("gq", 256), ("gk", 256), ("gv", 512), ("ga", GLA_RANK), ("gg", 512), ("mg", 4096)):
        cuts[name] = w_in[:, off:off + size]
        off += size

    def split_halves(w):
        return w.reshape(D_MODEL, HEADS, 2, 32).transpose(0, 2, 1, 3).reshape(D_MODEL, QK)

    w_main = jnp.concatenate([cuts["mg"], cuts["u"], split_halves(cuts["rq"]), split_halves(cuts["rk"]), cuts["rv"],
                              cuts["rg"], cuts["lx"], cuts["lg"], cuts["gq"], cuts["gk"], cuts["gv"], cuts["gg"]],
                             axis=1).astype(BF16)
    w_ga = jnp.pad(cuts["ga"], ((0, 0), (0, 128 - GLA_RANK))).astype(BF16)
    row = lambda v: v.reshape(1, -1)
    return {
        "norm_mix": row(lp["norm_mix"]), "w_main": w_main, "w_ga": w_ga,
        "wa2": jnp.pad(lp["gla_w_a2"], ((0, 128 - GLA_RANK), (0, 0))).astype(BF16), "ba2": row(lp["gla_b_a2"]),
        "s5": _s5_weights(lp["s5_lambda_re"], lp["s5_lambda_im"], lp["s5_log_dt"], lp["s5_b_re"], lp["s5_b_im"],
                          lp["s5_c_re"], lp["s5_c_im"]),
        "lru": {"conv_w": lp["lru_conv_w"], "conv_b": row(lp["lru_conv_b"]),
                "wa": _blockdiag(lp["lru_w_a"], LRU_BLOCKS)[0].astype(BF16), "ba": row(lp["lru_b_a"]),
                "wx": _blockdiag(lp["lru_w_x"], LRU_BLOCKS)[0].astype(BF16), "bx": row(lp["lru_b_x"]),
                "lam": row(lp["lru_lambda"])},
        "merge": {"s5_d": row(lp["s5_d"]), "w_glu": lp["s5_w_glu"].astype(BF16), "b_glu": row(lp["s5_b_glu"]),
                  "ret_norm": row(lp["ret_norm"]), "gla_norm": row(lp["gla_norm"]),
                  "w_branch": lp["w_branch"].astype(BF16), "w_out": lp["w_out"].astype(BF16),
                  "norm_ffn": row(lp["norm_ffn"])},
        "peer_wq": lp["peer_w_q"].astype(BF16),
        "peer_sk": lp["peer_sub_keys"].reshape(2 * PEER_HEADS, PEER_NK, PEER_NK).astype(BF16),
        "peer_u": lp["peer_u"].astype(BF16),
        "peer_v": lp["peer_v"].astype(BF16),
    }


def _token_tail(x, p, ys, ro, yc, go, w, gfin, final_norm, tm):
    h, n2 = _merge(x, p, ys, ro, yc, go, w["merge"], tm)
    st = _scores(n2, w["peer_wq"], w["peer_sk"], tm)
    return _peer(st, n2, h, w["peer_u"], w["peer_v"], gfin, tm, final_norm)


def _layer_prompt(x, w, rc, gc, rope, gfin, final_norm, bsz, seq, tm):
    t = bsz * seq
    p, ga = _inproj(x, w["norm_mix"], w["w_main"], w["w_ga"], math.gcd(t, INPROJ_ROWS))
    qr, kr, la = _prep(p, ga, rope[0], rope[1], w["wa2"], w["ba2"], tm)
    ys, s5_re, s5_im = _s5_prompt(p[:, COL_U:COL_U + MIX], w["s5"], bsz, seq)
    ro, s_ret = _ret_prompt(qr, kr, p, rc, bsz, seq)
    go, s_gla = _gla_prompt(p, la, gc, bsz, seq)
    to_tb = lambda a: a.reshape(bsz, seq, MIX).transpose(1, 0, 2).reshape(t, MIX)
    lx = p[:, COL_LX:COL_LX + MIX]
    yc, s_lru, _ = _lru(to_tb(lx), to_tb(p[:, COL_LG:COL_LG + MIX]), jnp.zeros((bsz, MIX), F32),
                        jnp.zeros(((CONV_W - 1) * bsz, MIX), F32), w["lru"], bsz, min(seq, 64))
    yc = yc.reshape(seq, bsz, MIX).transpose(1, 0, 2).reshape(t, MIX)
    s_conv = lx.reshape(bsz, seq, MIX)[:, seq - (CONV_W - 1):, :]
    out = _token_tail(x, p, ys, ro, yc, go, w, gfin, final_norm, tm)
    return out, (s5_re, s5_im, s_ret, s_lru, s_conv, s_gla)


def _layer_sample(x, st, w, rc, rope, gfin, final_norm):
    t = x.shape[0]
    s5_re0, s5_im0, ret0, lru0, conv0, gla0 = st
    p, ga = _inproj(x, w["norm_mix"], w["w_main"], w["w_ga"], t)
    qr, kr, la = _prep(p, ga, rope[0], rope[1], w["wa2"], w["ba2"], t)
    ys, s5_re, s5_im = _s5_step(p, s5_re0.reshape(t, -1), s5_im0.reshape(t, -1), w["s5"])
    ro, s_ret, go, s_gla = _mat_step(qr, kr, la, p, ret0, gla0, rc["rdec"])
    c0 = conv0.transpose(1, 0, 2).reshape((CONV_W - 1) * t, MIX)
    yc, s_lru, cf = _lru(p[:, COL_LX:COL_LX + MIX], p[:, COL_LG:COL_LG + MIX], lru0, c0, w["lru"], t, 1)
    s_conv = cf.reshape(CONV_W - 1, t, MIX).transpose(1, 0, 2)
    out = _token_tail(x, p, ys, ro, yc, go, w, gfin, final_norm, t)
    return out, (s5_re.reshape(s5_re0.shape), s5_im.reshape(s5_im0.shape), s_ret, s_lru, s_conv, s_gla)


_PARAM_NAMES = ("norm_mix", "w_in", "s5_lambda_re", "s5_lambda_im", "s5_log_dt", "s5_b_re", "s5_b_im", "s5_c_re",
                "s5_c_im", "s5_d", "s5_w_glu", "s5_b_glu", "ret_norm", "lru_conv_w", "lru_conv_b", "lru_w_a",
                "lru_b_a", "lru_w_x", "lru_b_x", "lru_lambda", "gla_w_a2", "gla_b_a2", "gla_norm", "w_branch",
                "w_out", "norm_ffn", "peer_w_q", "peer_sub_keys", "peer_u", "peer_v")


def _forward(x_prompt, x_sample, states, params, norm_final, tm):
    bsz, seq, _ = x_prompt.shape
    nsmp = x_sample.shape[0]
    depth = params["w_in"].shape[0]
    rc = _ret_consts()
    gc = _gla_consts()
    rope_p = _rope_tables(jnp.arange(seq, dtype=jnp.int32))
    rope_s = _rope_tables(jnp.full((nsmp,), PAST_LEN, dtype=jnp.int32))
    gfin = norm_final.reshape(1, -1)
    hp = x_prompt.reshape(bsz * seq, D_MODEL)
    hs = x_sample.reshape(nsmp, D_MODEL)
    new_p, new_s = [], []
    for l in range(depth):
        w = _layer_weights({k: v[l] for k, v in params.items()})
        last = l == depth - 1
        hp, sp = _layer_prompt(hp, w, rc, gc, rope_p, gfin, last, bsz, seq, tm)
        hs, ss = _layer_sample(hs, tuple(s[l] for s in states), w, rc, rope_s, gfin, last)
        new_p.append(sp)
        new_s.append(ss)
    outs = [hp.reshape(bsz, seq, D_MODEL), hs.reshape(nsmp, 1, D_MODEL)]
    outs += [jnp.stack([s[i] for s in new_p]) for i in range(6)]
    outs += [jnp.stack([s[i] for s in new_s]) for i in range(6)]
    return tuple(outs)


def kernel(x_prompt, x_sample, state_s5_re, state_s5_im, state_ret, state_lru, state_conv, state_gla, norm_mix, w_in, s5_lambda_re, s5_lambda_im, s5_log_dt, s5_b_re, s5_b_im, s5_c_re, s5_c_im, s5_d, s5_w_glu, s5_b_glu, ret_norm, lru_conv_w, lru_conv_b, lru_w_a, lru_b_a, lru_w_x, lru_b_x, lru_lambda, gla_w_a2, gla_b_a2, gla_norm, w_branch, w_out, norm_ffn, peer_w_q, peer_sub_keys, peer_u, peer_v, norm_final):
    values = (norm_mix, w_in, s5_lambda_re, s5_lambda_im, s5_log_dt, s5_b_re, s5_b_im, s5_c_re, s5_c_im, s5_d,
              s5_w_glu, s5_b_glu, ret_norm, lru_conv_w, lru_conv_b, lru_w_a, lru_b_a, lru_w_x, lru_b_x, lru_lambda,
              gla_w_a2, gla_b_a2, gla_norm, w_branch, w_out, norm_ffn, peer_w_q, peer_sub_keys, peer_u, peer_v)
    params = dict(zip(_PARAM_NAMES, values))
    states = (state_s5_re, state_s5_im, state_ret, state_lru, state_conv, state_gla)
    return _forward(x_prompt, x_sample, states, params, norm_final, 256)
```

```python
import functools
import math

import jax
import jax.numpy as jnp
from jax import lax
from jax.experimental import pallas as pl
from jax.experimental.pallas import tpu as pltpu

F32 = jnp.float32
BF16 = jnp.bfloat16

D_MODEL = 1024
MIX = 512
NORM_EPS = 1e-6
PAST_LEN = 16384
S5_GROUPS = 32
S5_GROUP = 16
S5_STATE = 64
HEADS = 4
DK = 64
DV = 128
QK = HEADS * DK
ROPE_BASE = 10000.0
RET_CHUNK = 256
GLA_CHUNK = 64
GLA_BLOCK = 256
GLA_RANK = 16
GLA_TAU = 16.0
LRU_BLOCKS = 8
LRU_C = 8.0
CONV_W = 4
PEER_HEADS = 8
PEER_NK = 128
PEER_EXPERTS = PEER_NK * PEER_NK
PEER_TOPK = 16
PEER_EBLK = 2048
PEER_ROWS = PEER_EBLK // PEER_NK

COL_MG = 0
COL_U = 4096
COL_RQ = 4608
COL_RK = 4864
COL_RV = 5120
COL_RG = 5632
COL_LX = 6144
COL_LG = 6656
COL_GQ = 7168
COL_GK = 7424
COL_GV = 7680
COL_GG = 8192
IN_MAIN = 8704
IN_TILE = IN_MAIN // 4
INPROJ_ROWS = 1024

VMEM_LIMIT = 52 * 1024 * 1024


def _cparams(sem):
    return pltpu.CompilerParams(dimension_semantics=sem, vmem_limit_bytes=VMEM_LIMIT)


def _dot(a, b):
    return jnp.dot(a, b, preferred_element_type=F32)


def _dot_nt(a, b):
    return lax.dot_general(a, b, (((1,), (1,)), ((), ())), preferred_element_type=F32)


def _dot_tn(a, b):
    return lax.dot_general(a, b, (((0,), (0,)), ((), ())), preferred_element_type=F32)


def _sigmoid(x):
    return 1.0 / (1.0 + jnp.exp(-x))


def _silu(x):
    return x * _sigmoid(x)


_GELU_C0 = -2.0 * math.sqrt(2.0 / math.pi) * math.log2(math.e)
_GELU_C1 = 0.044715 * _GELU_C0


def _gelu(x):
    return x / (1.0 + jnp.exp2(x * (_GELU_C1 * (x * x) + _GELU_C0)))


def _softplus(x):
    return jnp.maximum(x, 0.0) + jnp.log1p(jnp.exp(-jnp.abs(x)))


def _rms(x, g):
    return x * lax.rsqrt(jnp.mean(x * x, axis=-1, keepdims=True) + NORM_EPS) * g


def _inproj_kernel(x_ref, g_ref, w_ref, wga_ref, o_ref, ga_ref, n_ref):
    @pl.when(pl.program_id(1) == 0)
    def _():
        n = _rms(x_ref[...], g_ref[...]).astype(BF16)
        n_ref[...] = n
        ga_ref[...] = _dot(n, wga_ref[...])

    o_ref[...] = _dot(n_ref[...], w_ref[...])


def _inproj(x, g, w, wga, tm):
    t = x.shape[0]
    return pl.pallas_call(
        _inproj_kernel,
        out_shape=(jax.ShapeDtypeStruct((t, IN_MAIN), F32), jax.ShapeDtypeStruct((t, 128), F32)),
        grid=(t // tm, IN_MAIN // IN_TILE),
        in_specs=[pl.BlockSpec((tm, D_MODEL), lambda i, j: (i, 0)),
                  pl.BlockSpec((1, D_MODEL), lambda i, j: (0, 0)),
                  pl.BlockSpec((D_MODEL, IN_TILE), lambda i, j: (0, j)),
                  pl.BlockSpec((D_MODEL, 128), lambda i, j: (0, 0))],
        out_specs=(pl.BlockSpec((tm, IN_TILE), lambda i, j: (i, j)),
                   pl.BlockSpec((tm, 128), lambda i, j: (i, 0))),
        scratch_shapes=[pltpu.VMEM((tm, D_MODEL), BF16)],
        compiler_params=_cparams(("parallel", "arbitrary")),
        name="inproj",
    )(x, g, w, wga)


def _prep_kernel(rq_ref, rk_ref, ga_ref, cos_ref, sin_ref, wa2_ref, ba2_ref, qr_ref, kr_ref, la_ref):
    cos = cos_ref[...]
    sin = sin_ref[...]
    q = rq_ref[...]
    k = rk_ref[...]
    q1, q2 = q[:, :128], q[:, 128:]
    k1, k2 = k[:, :128], k[:, 128:]
    qr_ref[:, :128] = q1 * cos - q2 * sin
    qr_ref[:, 128:] = q1 * sin + q2 * cos
    kr_ref[:, :128] = (k1 * cos - k2 * sin) * (DK ** -0.5)
    kr_ref[:, 128:] = (k1 * sin + k2 * cos) * (DK ** -0.5)
    z = _dot(ga_ref[...].astype(BF16), wa2_ref[...]) + ba2_ref[...]
    la_ref[...] = (jnp.minimum(z, 0.0) - jnp.log1p(jnp.exp(-jnp.abs(z)))) * (1.0 / GLA_TAU)


def _prep(p, ga, cos, sin, wa2, ba2, tm):
    t = p.shape[0]
    n_pos = cos.shape[0] // tm
    return pl.pallas_call(
        _prep_kernel,
        out_shape=(jax.ShapeDtypeStruct((t, QK), F32),) * 3,
        grid=(t // tm,),
        in_specs=[pl.BlockSpec((tm, QK), lambda i: (i, COL_RQ // QK)),
                  pl.BlockSpec((tm, QK), lambda i: (i, COL_RK // QK)),
                  pl.BlockSpec((tm, 128), lambda i: (i, 0)),
                  pl.BlockSpec((tm, 128), lambda i: (i % n_pos, 0)),
                  pl.BlockSpec((tm, 128), lambda i: (i % n_pos, 0)),
                  pl.BlockSpec((128, QK), lambda i: (0, 0)),
                  pl.BlockSpec((1, QK), lambda i: (0, 0))],
        out_specs=(pl.BlockSpec((tm, QK), lambda i: (i, 0)),) * 3,
        compiler_params=_cparams(("parallel",)),
        name="prep",
    )(p, p, ga, cos, sin, wa2, ba2)


S5_SLABS = S5_GROUPS * S5_STATE // 128
S5_LANE_BLOCKS = MIX // 128
S5_SLABS_PER_BLOCK = S5_SLABS // S5_LANE_BLOCKS
S5_SCAN_SLABS = 4
SCAN_STEPS = 64


def _s5_prompt_kernel(u_ref, perm_ref, bre_ref, bim_ref, abr_ref, abi_ref, cre_ref, cim_ref,
                      y_ref, sr_ref, si_ref, wre, wim, st_re, st_im, *, tb, bsz):
    rows = bsz * tb
    spb = S5_SLABS_PER_BLOCK * 128

    @pl.when(pl.program_id(0) == 0)
    def _():
        st_re[...] = jnp.zeros_like(st_re)
        st_im[...] = jnp.zeros_like(st_im)

    perm = perm_ref[...]
    u = _dot(perm, u_ref[...].reshape(rows, MIX).astype(BF16)).astype(BF16)
    for lb in range(S5_LANE_BLOCKS):
        ul = u[:, lb * 128:(lb + 1) * 128]
        wre[:, lb * spb:(lb + 1) * spb] = _dot(ul, bre_ref[lb])
        wim[:, lb * spb:(lb + 1) * spb] = _dot(ul, bim_ref[lb])

    width = S5_SCAN_SLABS * 128
    for s0 in range(0, S5_SLABS * 128, width):
        lanes = slice(s0, s0 + width)
        a_re = jnp.broadcast_to(abr_ref[:, lanes], (bsz, width))
        a_im = jnp.broadcast_to(abi_ref[:, lanes], (bsz, width))

        def body(t, carry, lanes=lanes, a_re=a_re, a_im=a_im):
            s_re, s_im = carry
            sl = pl.ds(pl.multiple_of(t * bsz, bsz), bsz)
            n_re = a_re * s_re - a_im * s_im + wre[sl, lanes]
            n_im = a_re * s_im + a_im * s_re + wim[sl, lanes]
            wre[sl, lanes] = n_re
            wim[sl, lanes] = n_im
            return n_re, n_im

        s_re, s_im = lax.fori_loop(0, tb, body, (st_re[:, lanes], st_im[:, lanes]))
        st_re[:, lanes] = s_re
        st_im[:, lanes] = s_im
    sr_ref[...] = st_re[...]
    si_ref[...] = st_im[...]

    ys = []
    for lb in range(S5_LANE_BLOCKS):
        lanes = slice(lb * spb, (lb + 1) * spb)
        s_re = _dot_tn(perm, wre[:, lanes].astype(BF16)).astype(BF16)
        s_im = _dot_tn(perm, wim[:, lanes].astype(BF16)).astype(BF16)
        ys.append(_dot(s_re, cre_ref[lb]) - _dot(s_im, cim_ref[lb]))
    y_ref[...] = jnp.concatenate(ys, axis=-1).reshape(bsz, tb, MIX)


def _s5_prompt(p3, w, bsz, seq):
    tb = min(seq, SCAN_STEPS)
    rows = bsz * tb
    ns = S5_GROUPS * S5_STATE
    spb = S5_SLABS_PER_BLOCK * 128
    r = jnp.arange(rows)
    perm = ((r % bsz) * tb + r // bsz)[:, None] == r[None, :]
    c3 = lambda a, b, c: pl.BlockSpec((a, b, c), lambda i: (0, 0, 0))
    c2 = lambda a, b: pl.BlockSpec((a, b), lambda i: (0, 0))
    y, s_re, s_im = pl.pallas_call(
        functools.partial(_s5_prompt_kernel, tb=tb, bsz=bsz),
        out_shape=(jax.ShapeDtypeStruct((bsz, seq, MIX), F32),
                   jax.ShapeDtypeStruct((bsz, ns), F32), jax.ShapeDtypeStruct((bsz, ns), F32)),
        grid=(seq // tb,),
        in_specs=[pl.BlockSpec((bsz, tb, MIX), lambda i: (0, i, COL_U // MIX)), c2(rows, rows),
                  c3(S5_LANE_BLOCKS, 128, spb), c3(S5_LANE_BLOCKS, 128, spb), c2(1, ns), c2(1, ns),
                  c3(S5_LANE_BLOCKS, spb, 128), c3(S5_LANE_BLOCKS, spb, 128)],
        out_specs=(pl.BlockSpec((bsz, tb, MIX), lambda i: (0, i, 0)), c2(bsz, ns), c2(bsz, ns)),
        scratch_shapes=[pltpu.VMEM((rows, ns), F32), pltpu.VMEM((rows, ns), F32),
                        pltpu.VMEM((bsz, ns), F32), pltpu.VMEM((bsz, ns), F32)],
        compiler_params=_cparams(("arbitrary",)),
        name="s5_prompt",
    )(p3, perm.astype(BF16), w["bc_re"], w["bc_im"], w["ab_re"], w["ab_im"], w["cc_re"], w["cc_im"])
    to_state = lambda s: s.reshape(bsz, S5_GROUPS, S5_STATE)
    return y.reshape(bsz * seq, MIX), to_state(s_re), to_state(s_im)


def _s5_step_kernel(u_ref, hr_ref, hi_ref, bre_ref, bim_ref, abr_ref, abi_ref, cre_ref, cim_ref,
                    y_ref, sr_ref, si_ref):
    u = u_ref[...].astype(BF16)
    h_re = hr_ref[...]
    h_im = hi_ref[...]
    ab_re = abr_ref[...]
    ab_im = abi_ref[...]
    s_re = _dot(u, bre_ref[...]) + (ab_re * h_re - ab_im * h_im)
    s_im = _dot(u, bim_ref[...]) + (ab_re * h_im + ab_im * h_re)
    sr_ref[...] = s_re
    si_ref[...] = s_im
    y_ref[...] = _dot(s_re.astype(BF16), cre_ref[...]) - _dot(s_im.astype(BF16), cim_ref[...])


def _s5_step(p, h_re, h_im, w):
    t = p.shape[0]
    ns = S5_GROUPS * S5_STATE
    full = lambda a, b: pl.BlockSpec((a, b), lambda i: (0, 0))
    return pl.pallas_call(
        _s5_step_kernel,
        out_shape=(jax.ShapeDtypeStruct((t, MIX), F32), jax.ShapeDtypeStruct((t, ns), F32),
                   jax.ShapeDtypeStruct((t, ns), F32)),
        grid=(1,),
        in_specs=[pl.BlockSpec((t, MIX), lambda i: (0, COL_U // MIX)), full(t, ns), full(t, ns),
                  full(MIX, ns), full(MIX, ns), full(1, ns), full(1, ns), full(ns, MIX), full(ns, MIX)],
        out_specs=(full(t, MIX), full(t, ns), full(t, ns)),
        compiler_params=_cparams(("arbitrary",)),
        name="s5_step",
    )(p, h_re, h_im, w["bd_re"], w["bd_im"], w["ab_re"], w["ab_im"], w["cd_re"], w["cd_im"])


def _lru_coeffs(xc, wa_ref, ba_ref, wx_ref, bx_ref, lam_ref):
    xcb = xc.astype(BF16)
    r = _sigmoid(_dot(xcb, wa_ref[...]) + ba_ref[...])
    i = _sigmoid(_dot(xcb, wx_ref[...]) + bx_ref[...])
    log_a = (-LRU_C) * r * _softplus(-lam_ref[...])
    a = jnp.exp(log_a)
    return a, jnp.sqrt(-jnp.tanh(log_a) * (a * a + 1.0)) * (i * xc)


LRU_SLABS = MIX // 128
CONV_HALO = 8


def _lru_prompt_kernel(x_ref, g_ref, cw_ref, cb_ref, wa_ref, ba_ref, wx_ref, bx_ref, lam_ref,
                       y_ref, hf_ref, xbuf, hs, abuf, bbuf, *, tb, bsz):
    rows = bsz * tb

    @pl.when(pl.program_id(0) == 0)
    def _():
        xbuf[:, 0:CONV_HALO, :] = jnp.zeros((bsz, CONV_HALO, MIX), F32)
        hs[...] = jnp.zeros_like(hs)

    xbuf[:, CONV_HALO:CONV_HALO + tb, :] = x_ref[...]
    first = CONV_HALO - (CONV_W - 1)
    xc = cb_ref[...] + xbuf[:, first:first + tb, :] * cw_ref[0:1, :]
    for j in range(1, CONV_W):
        xc = xc + xbuf[:, first + j:first + j + tb, :] * cw_ref[j:j + 1, :]
    a, b = _lru_coeffs(xc.reshape(rows, MIX), wa_ref, ba_ref, wx_ref, bx_ref, lam_ref)
    for l in range(LRU_SLABS):
        abuf[l] = a[:, l * 128:(l + 1) * 128]
        bbuf[l] = b[:, l * 128:(l + 1) * 128]

    def body(t, hcar):
        sl = pl.ds(t, bsz, stride=tb)
        out = []
        for l in range(LRU_SLABS):
            h = abuf[l, sl, :] * hcar[l] + bbuf[l, sl, :]
            bbuf[l, sl, :] = h
            out.append(h)
        return tuple(out)

    fin = lax.fori_loop(0, tb, body, tuple(hs[l] for l in range(LRU_SLABS)))
    for l in range(LRU_SLABS):
        hs[l] = fin[l]
    hf_ref[...] = hs[...]
    h_all = jnp.concatenate([bbuf[l] for l in range(LRU_SLABS)], axis=-1)
    y_ref[...] = (h_all * _gelu(g_ref[...].reshape(rows, MIX))).reshape(bsz, tb, MIX)
    xbuf[:, first:CONV_HALO, :] = xbuf[:, first + tb:CONV_HALO + tb, :]


def _lru_prompt(p3, w, bsz, seq):
    tb = min(seq, SCAN_STEPS)
    full = lambda a, b: pl.BlockSpec((a, b), lambda i: (0, 0))
    y, hf = pl.pallas_call(
        functools.partial(_lru_prompt_kernel, tb=tb, bsz=bsz),
        out_shape=(jax.ShapeDtypeStruct((bsz, seq, MIX), F32), jax.ShapeDtypeStruct((LRU_SLABS, bsz, 128), F32)),
        grid=(seq // tb,),
        in_specs=[pl.BlockSpec((bsz, tb, MIX), lambda i: (0, i, COL_LX // MIX)),
                  pl.BlockSpec((bsz, tb, MIX), lambda i: (0, i, COL_LG // MIX)),
                  full(CONV_W, MIX), full(1, MIX), full(MIX, MIX), full(1, MIX), full(MIX, MIX), full(1, MIX),
                  full(1, MIX)],
        out_specs=(pl.BlockSpec((bsz, tb, MIX), lambda i: (0, i, 0)),
                   pl.BlockSpec((LRU_SLABS, bsz, 128), lambda i: (0, 0, 0))),
        scratch_shapes=[pltpu.VMEM((bsz, CONV_HALO + tb, MIX), F32), pltpu.VMEM((LRU_SLABS, bsz, 128), F32),
                        pltpu.VMEM((LRU_SLABS, bsz * tb, 128), F32), pltpu.VMEM((LRU_SLABS, bsz * tb, 128), F32)],
        compiler_params=_cparams(("arbitrary",)),
        name="rglru_prompt",
    )(p3, p3, w["conv_w"], w["conv_b"], w["wa"], w["ba"], w["wx"], w["bx"], w["lam"])
    return y.reshape(bsz * seq, MIX), hf.transpose(1, 0, 2).reshape(bsz, MIX)


def _lru_kernel(x_ref, g_ref, h0_ref, c0_ref, cw_ref, cb_ref, wa_ref, ba_ref, wx_ref, bx_ref, lam_ref,
                y_ref, hf_ref, cf_ref, xbuf, hs, abuf, bbuf, *, steps, bsz):
    rows = steps * bsz
    halo = (CONV_W - 1) * bsz

    @pl.when(pl.program_id(0) == 0)
    def _():
        xbuf[0:halo, :] = c0_ref[...]
        hs[...] = h0_ref[...]

    xbuf[halo:halo + rows, :] = x_ref[...]
    xc = cb_ref[...] + xbuf[0:rows, :] * cw_ref[0:1, :]
    for j in range(1, CONV_W):
        xc = xc + xbuf[j * bsz:j * bsz + rows, :] * cw_ref[j:j + 1, :]
    a, b = _lru_coeffs(xc, wa_ref, ba_ref, wx_ref, bx_ref, lam_ref)
    abuf[...] = a
    bbuf[...] = b

    def body(t, h):
        sl = pl.ds(pl.multiple_of(t * bsz, bsz), bsz)
        h = abuf[sl, :] * h + bbuf[sl, :]
        bbuf[sl, :] = h
        return h

    h = lax.fori_loop(0, steps, body, hs[...])
    hs[...] = h
    hf_ref[...] = h
    y_ref[...] = bbuf[...] * _gelu(g_ref[...])
    tail = xbuf[rows:rows + halo, :]
    xbuf[0:halo, :] = tail
    cf_ref[...] = tail


def _lru(x, g, h0, c0, w, bsz, steps):
    t = x.shape[0]
    rows = steps * bsz
    halo = (CONV_W - 1) * bsz
    full = lambda a, b: pl.BlockSpec((a, b), lambda i: (0, 0))
    return pl.pallas_call(
        functools.partial(_lru_kernel, steps=steps, bsz=bsz),
        out_shape=(jax.ShapeDtypeStruct((t, MIX), F32), jax.ShapeDtypeStruct((bsz, MIX), F32),
                   jax.ShapeDtypeStruct((halo, MIX), F32)),
        grid=(t // rows,),
        in_specs=[pl.BlockSpec((rows, MIX), lambda i: (i, 0)), pl.BlockSpec((rows, MIX), lambda i: (i, 0)),
                  full(bsz, MIX), full(halo, MIX), full(CONV_W, MIX), full(1, MIX),
                  full(MIX, MIX), full(1, MIX), full(MIX, MIX), full(1, MIX), full(1, MIX)],
        out_specs=(pl.BlockSpec((rows, MIX), lambda i: (i, 0)), full(bsz, MIX), full(halo, MIX)),
        scratch_shapes=[pltpu.VMEM((rows + halo, MIX), F32), pltpu.VMEM((bsz, MIX), F32),
                        pltpu.VMEM((rows, MIX), F32), pltpu.VMEM((rows, MIX), F32)],
        compiler_params=_cparams(("arbitrary",)),
        name="rglru",
    )(x, g, h0, c0, w["conv_w"], w["conv_b"], w["wa"], w["ba"], w["wx"], w["bx"], w["lam"])


def _ret_prompt_kernel(q_ref, k_ref, v_ref, dec_ref, wq_ref, wk_ref, gc_ref, hm_ref, o_ref, sf_ref, s_ref):
    @pl.when(pl.program_id(1) == 0)
    def _():
        s_ref[...] = jnp.zeros_like(s_ref)

    q = q_ref[...]
    k = k_ref[...]
    kb = k.astype(BF16)
    vb = v_ref[...].astype(BF16)
    sb = s_ref[...].astype(BF16)
    qw = q * wq_ref[...]
    for h in range(HEADS):
        hm = hm_ref[h]
        vs = slice(h * DV, (h + 1) * DV)
        sc = _dot_nt((q * hm).astype(BF16), kb) * dec_ref[h]
        o_ref[:, vs] = _dot(sc.astype(BF16), vb[:, vs]) + _dot((qw * hm).astype(BF16), sb[:, vs])
    s_new = s_ref[...] * gc_ref[...] + _dot_tn((k * wk_ref[...]).astype(BF16), vb)
    s_ref[...] = s_new
    sf_ref[0] = s_new


def _ret_prompt(qr, kr, p, c, bsz, seq):
    ch = RET_CHUNK
    n = seq // ch
    full2 = lambda a, b: pl.BlockSpec((a, b), lambda bi, ci: (0, 0))
    o, sf = pl.pallas_call(
        _ret_prompt_kernel,
        out_shape=(jax.ShapeDtypeStruct((bsz * seq, MIX), F32), jax.ShapeDtypeStruct((bsz, QK, MIX), F32)),
        grid=(bsz, n),
        in_specs=[pl.BlockSpec((ch, QK), lambda bi, ci: (bi * n + ci, 0)),
                  pl.BlockSpec((ch, QK), lambda bi, ci: (bi * n + ci, 0)),
                  pl.BlockSpec((ch, MIX), lambda bi, ci: (bi * n + ci, COL_RV // MIX)),
                  pl.BlockSpec((HEADS, ch, ch), lambda bi, ci: (0, 0, 0)),
                  full2(ch, QK), full2(ch, QK), full2(1, MIX),
                  pl.BlockSpec((HEADS, 1, QK), lambda bi, ci: (0, 0, 0))],
        out_specs=(pl.BlockSpec((ch, MIX), lambda bi, ci: (bi * n + ci, 0)),
                   pl.BlockSpec((1, QK, MIX), lambda bi, ci: (bi, 0, 0))),
        scratch_shapes=[pltpu.VMEM((QK, MIX), F32)],
        compiler_params=_cparams(("parallel", "arbitrary")),
        name="ret_prompt",
    )(qr, kr, p, c["dec"], c["wq"], c["wk"], c["gc"], c["hmask"])
    s5d = sf.reshape(bsz, 2, HEADS, 32, HEADS, DV)
    st = jnp.stack([s5d[:, :, h, :, h, :] for h in range(HEADS)], axis=1)
    return o, st.reshape(bsz, HEADS, DK, DV)


def _gla_prompt_kernel(q_ref, k_ref, v_ref, la_ref, tril_ref, hm_ref, o_ref, sf_ref, s_ref):
    @pl.when(pl.program_id(1) == 0)
    def _():
        s_ref[...] = jnp.zeros_like(s_ref)

    tril = tril_ref[...]
    for c in range(GLA_BLOCK // GLA_CHUNK):
        rs = slice(c * GLA_CHUNK, (c + 1) * GLA_CHUNK)
        b = jnp.dot(tril, la_ref[rs, :], preferred_element_type=F32, precision=lax.Precision.HIGHEST)
        b_last = b[GLA_CHUNK - 1:GLA_CHUNK, :]
        q = q_ref[rs, :] * (DK ** -0.5)
        k = k_ref[rs, :]
        q_t = q * jnp.exp(b)
        k_t = (k * jnp.exp(-b)).astype(BF16)
        k_s = (k * jnp.exp(b_last - b)).astype(BF16)
        vb = v_ref[rs, :].astype(BF16)
        sb = s_ref[...].astype(BF16)
        for h in range(HEADS):
            qm = (q_t * hm_ref[h]).astype(BF16)
            vs = slice(h * DV, (h + 1) * DV)
            sc = jnp.where(tril > 0.0, _dot_nt(qm, k_t), 0.0)
            o_ref[rs, vs] = _dot(sc.astype(BF16), vb[:, vs]) + _dot_nt(qm, sb[vs, :])
        s_ref[...] = s_ref[...] * jnp.exp(b_last) + _dot_tn(vb, k_s)
    sf_ref[0] = s_ref[...]


def _gla_prompt(p, la, c, bsz, seq):
    blk = GLA_BLOCK
    n = seq // blk
    o, sf = pl.pallas_call(
        _gla_prompt_kernel,
        out_shape=(jax.ShapeDtypeStruct((bsz * seq, MIX), F32), jax.ShapeDtypeStruct((bsz, MIX, QK), F32)),
        grid=(bsz, n),
        in_specs=[pl.BlockSpec((blk, QK), lambda bi, ci: (bi * n + ci, COL_GQ // QK)),
                  pl.BlockSpec((blk, QK), lambda bi, ci: (bi * n + ci, COL_GK // QK)),
                  pl.BlockSpec((blk, MIX), lambda bi, ci: (bi * n + ci, COL_GV // MIX)),
                  pl.BlockSpec((blk, QK), lambda bi, ci: (bi * n + ci, 0)),
                  pl.BlockSpec((GLA_CHUNK, GLA_CHUNK), lambda bi, ci: (0, 0)),
                  pl.BlockSpec((HEADS, 1, QK), lambda bi, ci: (0, 0, 0))],
        out_specs=(pl.BlockSpec((blk, MIX), lambda bi, ci: (bi * n + ci, 0)),
                   pl.BlockSpec((1, MIX, QK), lambda bi, ci: (bi, 0, 0))),
        scratch_shapes=[pltpu.VMEM((MIX, QK), F32)],
        compiler_params=_cparams(("parallel", "arbitrary")),
        name="gla_prompt",
    )(p, p, p, la, c["tril"], c["hmask"])
    s5d = sf.reshape(bsz, HEADS, DV, HEADS, DK)
    st = jnp.stack([s5d[:, h, :, h, :] for h in range(HEADS)], axis=1)
    return o, st.transpose(0, 1, 3, 2)


STEP_GROUP = 8


def _mat_step_kernel(rq_ref, rk_ref, rv_ref, rs_ref, gq_ref, gk_ref, ga_ref, gv_ref, gs_ref, rdec_ref,
                     ro_ref, rso_ref, go_ref, gso_ref):
    rq = rq_ref[0]
    rk = rk_ref[0]
    gq = gq_ref[0] * (DK ** -0.5)
    gk = gk_ref[0]
    ga = jnp.exp(ga_ref[0])
    for i in range(STEP_GROUP):
        for h in range(HEADS):
            ks = slice(h * DK, (h + 1) * DK)
            vs = slice(h * DV, (h + 1) * DV)
            s1 = rs_ref[i, h] * rdec_ref[h] + rk[ks, i:i + 1] * rv_ref[i:i + 1, vs]
            rso_ref[i, h] = s1
            ro_ref[i:i + 1, vs] = jnp.sum(rq[ks, i:i + 1] * s1, axis=0, keepdims=True)
            s1 = gs_ref[i, h] * ga[ks, i:i + 1] + gk[ks, i:i + 1] * gv_ref[i:i + 1, vs]
            gso_ref[i, h] = s1
            go_ref[i:i + 1, vs] = jnp.sum(gq[ks, i:i + 1] * s1, axis=0, keepdims=True)


def _mat_step(qr, kr, la, p, s_ret, s_gla, rdec):
    t = p.shape[0]
    g = STEP_GROUP
    ng = t // g

    def tr(x):
        return x.reshape(ng, g, QK).transpose(0, 2, 1)

    def unsplit(x):
        return x.reshape(t, 2, HEADS, 32).transpose(0, 2, 1, 3).reshape(t, QK)

    gq = p[:, COL_GQ:COL_GQ + QK]
    gk = p[:, COL_GK:COL_GK + QK]
    tspec = pl.BlockSpec((1, QK, g), lambda i: (i, 0, 0))
    sspec = pl.BlockSpec((g, HEADS, DK, DV), lambda i: (i, 0, 0, 0))
    vspec = lambda col: pl.BlockSpec((g, MIX), lambda i: (i, col // MIX))
    ospec = pl.BlockSpec((g, MIX), lambda i: (i, 0))
    return pl.pallas_call(
        _mat_step_kernel,
        out_shape=(jax.ShapeDtypeStruct((t, MIX), F32), jax.ShapeDtypeStruct(s_ret.shape, F32),
                   jax.ShapeDtypeStruct((t, MIX), F32), jax.ShapeDtypeStruct(s_gla.shape, F32)),
        grid=(ng,),
        in_specs=[tspec, tspec, vspec(COL_RV), sspec, tspec, tspec, tspec, vspec(COL_GV), sspec,
                  pl.BlockSpec((HEADS, 1, DV), lambda i: (0, 0, 0))],
        out_specs=(ospec, sspec, ospec, sspec),
        compiler_params=_cparams(("parallel",)),
        name="mat_step",
    )(tr(unsplit(qr)), tr(unsplit(kr)), p, s_ret, tr(gq), tr(gk), tr(la), p, s_gla, rdec)


def _head_norm(o, g):
    parts = []
    for h in range(HEADS):
        oh = o[:, h * DV:(h + 1) * DV]
        mu = jnp.mean(oh, axis=-1, keepdims=True)
        d = oh - mu
        var = jnp.mean(d * d, axis=-1, keepdims=True)
        parts.append(d * lax.rsqrt(var + NORM_EPS))
    return jnp.concatenate(parts, axis=-1) * g


def _merge_kernel(x_ref, mg_ref, u_ref, rg_ref, gg_ref, ys_ref, ro_ref, yc_ref, go_ref,
                  d_ref, wglu_ref, bglu_ref, gret_ref, ggla_ref, wb_ref, wout_ref, gffn_ref,
                  h_ref, n_ref):
    ya = _gelu(ys_ref[...] + d_ref[...] * u_ref[...])
    ya = ya * _sigmoid(_dot(ya.astype(BF16), wglu_ref[...]) + bglu_ref[...])
    yb = _silu(rg_ref[...]) * _head_norm(ro_ref[...], gret_ref[...])
    yc = yc_ref[...]
    yd = _silu(gg_ref[...]) * _head_norm(go_ref[...], ggla_ref[...])
    m = None
    for i, y in enumerate((ya, yb, yc, yd)):
        z = _dot(y.astype(BF16), wb_ref[i])
        gz = _sigmoid(mg_ref[:, i * D_MODEL:(i + 1) * D_MODEL]) * z
        m = gz if m is None else m + gz
    h = x_ref[...] + _dot(m.astype(BF16), wout_ref[...])
    h_ref[...] = h
    n_ref[...] = _rms(h, gffn_ref[...]).astype(BF16)


def _merge(x, p, ys, ro, yc, go, w, tm):
    t = x.shape[0]
    row = lambda width, col=0: pl.BlockSpec((tm, width), lambda i: (i, col // width))
    full = lambda a, b: pl.BlockSpec((a, b), lambda i: (0, 0))
    return pl.pallas_call(
        _merge_kernel,
        out_shape=(jax.ShapeDtypeStruct((t, D_MODEL), F32), jax.ShapeDtypeStruct((t, D_MODEL), BF16)),
        grid=(t // tm,),
        in_specs=[row(D_MODEL), row(4 * D_MODEL, COL_MG), row(MIX, COL_U), row(MIX, COL_RG), row(MIX, COL_GG),
                  row(MIX), row(MIX), row(MIX), row(MIX),
                  full(1, MIX), full(MIX, MIX), full(1, MIX), full(1, MIX), full(1, MIX),
                  pl.BlockSpec((4, MIX, D_MODEL), lambda i: (0, 0, 0)), full(D_MODEL, D_MODEL), full(1, D_MODEL)],
        out_specs=(row(D_MODEL), row(D_MODEL)),
        compiler_params=_cparams(("parallel",)),
        name="merge",
    )(x, p, p, p, p, ys, ro, yc, go, w["s5_d"], w["w_glu"], w["b_glu"], w["ret_norm"], w["gla_norm"],
      w["w_branch"], w["w_out"], w["norm_ffn"])


def _scores_kernel(n_ref, wq_ref, sk_ref, st_ref):
    q = _dot(n_ref[...], wq_ref[...]).astype(BF16)
    for hp in range(2 * PEER_HEADS):
        st_ref[hp] = _dot_nt(sk_ref[hp], q[:, hp * PEER_NK:(hp + 1) * PEER_NK])


def _scores(n2, wq, sk, tm):
    t = n2.shape[0]
    return pl.pallas_call(
        _scores_kernel,
        out_shape=jax.ShapeDtypeStruct((2 * PEER_HEADS, PEER_NK, t), F32),
        grid=(t // tm,),
        in_specs=[pl.BlockSpec((tm, D_MODEL), lambda i: (i, 0)),
                  pl.BlockSpec((D_MODEL, 2 * PEER_HEADS * PEER_NK), lambda i: (0, 0)),
                  pl.BlockSpec((2 * PEER_HEADS, PEER_NK, PEER_NK), lambda i: (0, 0, 0))],
        out_specs=pl.BlockSpec((2 * PEER_HEADS, PEER_NK, tm), lambda i: (0, 0, i)),
        compiler_params=_cparams(("parallel",)),
        name="peer_scores",
    )(n2, wq, sk)


PEER_NTOP = PEER_TOPK + 1
PEER_CAND_SPANS = tuple(PEER_NTOP // (a + 1) for a in range(PEER_NTOP))
PEER_NCAND = sum(PEER_CAND_SPANS)
PEER_CAND_ROWS = -(-PEER_NCAND // 8) * 8


def _top_desc(cur, n, dst_ref=None):
    vals = []
    for r in range(n):
        m = jnp.max(cur, axis=0, keepdims=True)
        vals.append(m)
        if dst_ref is not None:
            dst_ref[r:r + 1, :] = m
        if r + 1 < n:
            cur = jnp.where(cur >= m, -jnp.inf, cur)
    return vals


def _peer_prologue(st_ref, e0_ref, e1_ref, tau_ref, a1_ref, cand_ref):
    cand_ref[...] = jnp.full(cand_ref.shape, -jnp.inf, F32)
    for h in range(PEER_HEADS):
        s0 = st_ref[2 * h]
        s1 = st_ref[2 * h + 1]
        a0 = _top_desc(s0, PEER_NTOP)
        a1 = _top_desc(s1, PEER_NTOP, a1_ref)
        off = 0
        for a, span in enumerate(PEER_CAND_SPANS):
            cand_ref[off:off + span, :] = a0[a] + a1_ref[0:span, :]
            off += span
        cand = cand_ref[...]
        top = _top_desc(cand, PEER_NTOP)
        tau = 0.5 * (top[PEER_TOPK - 1] + top[PEER_TOPK])
        z = jnp.sum(jnp.where(cand >= tau, jnp.exp(cand - top[0]), 0.0), axis=0, keepdims=True)
        e0_ref[h] = jnp.exp(s0 - a0[0])
        e1_ref[h] = jnp.exp(s1 - a1[0]) / z
        tau_ref[h] = tau


def _peer_kernel(st_ref, n_ref, h_ref, u_ref, vp_ref, vl_ref, gf_ref, o_ref,
                 e0_ref, e1_ref, tau_ref, a1_ref, cand_ref, hid_ref, a_ref, acc_ref, *, final_norm):
    j = pl.program_id(1)

    @pl.when(j == 0)
    def _():
        acc_ref[...] = jnp.zeros_like(acc_ref)
        a_ref[...] = jnp.zeros_like(a_ref)
        _peer_prologue(st_ref, e0_ref, e1_ref, tau_ref, a1_ref, cand_ref)

    cur = j % 2
    hid_ref[...] = _dot_nt(u_ref[...], n_ref[...])
    acc_ref[...] += _dot_tn(a_ref[1 - cur], vp_ref[...])
    for r in range(PEER_ROWS):
        i0 = j * PEER_ROWS + r
        g = None
        for h in range(PEER_HEADS):
            theta = tau_ref[h] - st_ref[2 * h, pl.ds(i0, 1), :]
            w = jnp.where(st_ref[2 * h + 1] >= theta, e1_ref[h], 0.0) * e0_ref[h, pl.ds(i0, 1), :]
            g = w if g is None else g + w
        rows = slice(r * PEER_NK, (r + 1) * PEER_NK)
        a_ref[cur, rows, :] = (g * _gelu(hid_ref[rows, :])).astype(BF16)

    @pl.when(j == pl.num_programs(1) - 1)
    def _():
        out = h_ref[...] + acc_ref[...] + _dot_tn(a_ref[cur], vl_ref[...])
        if final_norm:
            out = _rms(out, gf_ref[...])
        o_ref[...] = out


def _peer(st, n2, h, u, v, gfin, tm, final_norm):
    t = n2.shape[0]
    nj = PEER_EXPERTS // PEER_EBLK
    return pl.pallas_call(
        functools.partial(_peer_kernel, final_norm=final_norm),
        out_shape=jax.ShapeDtypeStruct((t, D_MODEL), F32),
        grid=(t // tm, nj),
        in_specs=[pl.BlockSpec((2 * PEER_HEADS, PEER_NK, tm), lambda i, j: (0, 0, i)),
                  pl.BlockSpec((tm, D_MODEL), lambda i, j: (i, 0)),
                  pl.BlockSpec((tm, D_MODEL), lambda i, j: (i, 0)),
                  pl.BlockSpec((PEER_EBLK, D_MODEL), lambda i, j: (j, 0)),
                  pl.BlockSpec((PEER_EBLK, D_MODEL), lambda i, j: (jnp.maximum(j - 1, 0), 0)),
                  pl.BlockSpec((PEER_EBLK, D_MODEL), lambda i, j: (nj - 1, 0)),
                  pl.BlockSpec((1, D_MODEL), lambda i, j: (0, 0))],
        out_specs=pl.BlockSpec((tm, D_MODEL), lambda i, j: (i, 0)),
        scratch_shapes=[pltpu.VMEM((PEER_HEADS, PEER_NK, tm), F32), pltpu.VMEM((PEER_HEADS, PEER_NK, tm), F32),
                        pltpu.VMEM((PEER_HEADS, 1, tm), F32),
                        pltpu.VMEM((PEER_NTOP + 7, tm), F32), pltpu.VMEM((PEER_CAND_ROWS, tm), F32),
                        pltpu.VMEM((PEER_EBLK, tm), F32), pltpu.VMEM((2, PEER_EBLK, tm), BF16),
                        pltpu.VMEM((tm, D_MODEL), F32)],
        compiler_params=_cparams(("parallel", "arbitrary")),
        name="peer",
    )(st, n2, h, u, v, v, gfin)


def _blockdiag(x, n):
    nm, r, c = x.shape
    m = nm // n
    return jnp.einsum("mgrc,gh->mgrhc", x.reshape(m, n, r, c), jnp.eye(n, dtype=x.dtype)).reshape(m, n * r, n * c)


def _s5_weights(lam_re, lam_im, log_dt, b_re, b_im, c_re, c_im):
    dt = jnp.exp(log_dt)[:, None]
    mag = jnp.exp(lam_re * dt)
    ab_re = mag * jnp.cos(lam_im * dt)
    ab_im = mag * jnp.sin(lam_im * dt)
    den = lam_re * lam_re + lam_im * lam_im
    zr = ((ab_re - 1.0) * lam_re + ab_im * lam_im) / den
    zi = (ab_im * lam_re - (ab_re - 1.0) * lam_im) / den
    bb_re = zr[..., None] * b_re - zi[..., None] * b_im
    bb_im = zr[..., None] * b_im + zi[..., None] * b_re
    eye = jnp.eye(S5_GROUPS, dtype=F32)
    gpb = S5_GROUPS // S5_LANE_BLOCKS
    b_blocks = lambda b: _blockdiag(b.transpose(0, 2, 1), gpb).astype(BF16)
    c_blocks = lambda c: _blockdiag(c.transpose(0, 2, 1), gpb).astype(BF16)
    return {
        "ab_re": ab_re.reshape(1, -1), "ab_im": ab_im.reshape(1, -1),
        "bc_re": b_blocks(bb_re), "bc_im": b_blocks(bb_im), "cc_re": c_blocks(c_re), "cc_im": c_blocks(c_im),
        "bd_re": jnp.einsum("gpc,gh->gchp", bb_re, eye).reshape(MIX, -1).astype(BF16),
        "bd_im": jnp.einsum("gpc,gh->gchp", bb_im, eye).reshape(MIX, -1).astype(BF16),
        "cd_re": jnp.einsum("gcp,gh->gphc", c_re, eye).reshape(-1, MIX).astype(BF16),
        "cd_im": jnp.einsum("gcp,gh->gphc", c_im, eye).reshape(-1, MIX).astype(BF16),
    }


def _ret_consts():
    ch = RET_CHUNK
    log_g = jnp.log1p(-jnp.exp2(-5.0 - jnp.arange(HEADS, dtype=F32)))
    idx = jnp.arange(ch, dtype=F32)
    diff = idx[:, None] - idx[None, :]
    mask = diff >= 0
    dec = jnp.where(mask[None], jnp.exp(log_g[:, None, None] * jnp.where(mask, diff, 0.0)[None]), 0.0)
    lane_head = (jnp.arange(QK) % 128) // 32
    lg_lane = log_g[lane_head]
    wq = jnp.exp(lg_lane[None, :] * (idx + 1.0)[:, None])
    wk = jnp.exp(lg_lane[None, :] * (ch - 1.0 - idx)[:, None])
    gc = jnp.repeat(jnp.exp(log_g * ch), DV)[None, :]
    hmask = (lane_head[None, :] == jnp.arange(HEADS)[:, None]).astype(F32)[:, None, :]
    rdec = jnp.broadcast_to(jnp.exp(log_g)[:, None, None], (HEADS, 1, DV))
    return {"dec": dec, "wq": wq, "wk": wk, "gc": gc, "hmask": hmask, "rdec": rdec}


def _gla_consts():
    tril = jnp.tril(jnp.ones((GLA_CHUNK, GLA_CHUNK), F32))
    hmask = ((jnp.arange(QK) // DK)[None, :] == jnp.arange(HEADS)[:, None]).astype(F32)[:, None, :]
    return {"tril": tril, "hmask": hmask}


def _rope_tables(pos):
    half = DK // 2
    inv = ROPE_BASE ** (-jnp.arange(half, dtype=F32) / half)
    ang = pos.astype(F32)[:, None] * inv[None, :]
    return jnp.tile(jnp.cos(ang), (1, HEADS)), jnp.tile(jnp.sin(ang), (1, HEADS))


def _layer_weights(lp):
    w_in = lp["w_in"]
    cuts = {}
    off = 0
    for name, size in (("u", 512), ("rq", 256), ("rk", 256), ("rv", 512), ("rg", 512), ("lx", 512), ("lg", 512),
                       ("gq", 256), ("gk", 256), ("gv", 512), ("ga", GLA_RANK), ("gg", 512), ("mg", 4096)):
        cuts[name] = w_in[:, off:off + size]
        off += size

    def split_halves(w):
        return w.reshape(D_MODEL, HEADS, 2, 32).transpose(0, 2, 1, 3).reshape(D_MODEL, QK)

    w_main = jnp.concatenate([cuts["mg"], cuts["u"], split_halves(cuts["rq"]), split_halves(cuts["rk"]), cuts["rv"],
                              cuts["rg"], cuts["lx"], cuts["lg"], cuts["gq"], cuts["gk"], cuts["gv"], cuts["gg"]],
                             axis=1).astype(BF16)
    w_ga = jnp.pad(cuts["ga"], ((0, 0), (0, 128 - GLA_RANK))).astype(BF16)
    row = lambda v: v.reshape(1, -1)
    return {
        "norm_mix": row(lp["norm_mix"]), "w_main": w_main, "w_ga": w_ga,
        "wa2": jnp.pad(lp["gla_w_a2"], ((0, 128 - GLA_RANK), (0, 0))).astype(BF16), "ba2": row(lp["gla_b_a2"]),
        "s5": _s5_weights(lp["s5_lambda_re"], lp["s5_lambda_im"], lp["s5_log_dt"], lp["s5_b_re"], lp["s5_b_im"],
                          lp["s5_c_re"], lp["s5_c_im"]),
        "lru": {"conv_w": lp["lru_conv_w"], "conv_b": row(lp["lru_conv_b"]),
                "wa": _blockdiag(lp["lru_w_a"], LRU_BLOCKS)[0].astype(BF16), "ba": row(lp["lru_b_a"]),
                "wx": _blockdiag(lp["lru_w_x"], LRU_BLOCKS)[0].astype(BF16), "bx": row(lp["lru_b_x"]),
                "lam": row(lp["lru_lambda"])},
        "merge": {"s5_d": row(lp["s5_d"]), "w_glu": lp["s5_w_glu"].astype(BF16), "b_glu": row(lp["s5_b_glu"]),
                  "ret_norm": row(lp["ret_norm"]), "gla_norm": row(lp["gla_norm"]),
                  "w_branch": lp["w_branch"].astype(BF16), "w_out": lp["w_out"].astype(BF16),
                  "norm_ffn": row(lp["norm_ffn"])},
        "peer_wq": lp["peer_w_q"].astype(BF16),
        "peer_sk": lp["peer_sub_keys"].reshape(2 * PEER_HEADS, PEER_NK, PEER_NK).astype(BF16),
        "peer_u": lp["peer_u"].astype(BF16),
        "peer_v": lp["peer_v"].astype(BF16),
    }


def _token_tail(x, p, ys, ro, yc, go, w, gfin, final_norm, tm):
    h, n2 = _merge(x, p, ys, ro, yc, go, w["merge"], tm)
    st = _scores(n2, w["peer_wq"], w["peer_sk"], tm)
    return _peer(st, n2, h, w["peer_u"], w["peer_v"], gfin, tm, final_norm)


def _layer_prompt(x, w, rc, gc, rope, gfin, final_norm, bsz, seq, tm):
    t = bsz * seq
    p, ga = _inproj(x, w["norm_mix"], w["w_main"], w["w_ga"], math.gcd(t, INPROJ_ROWS))
    qr, kr, la = _prep(p, ga, rope[0], rope[1], w["wa2"], w["ba2"], tm)
    p3 = p.reshape(bsz, seq, IN_MAIN)
    ys, s5_re, s5_im = _s5_prompt(p3, w["s5"], bsz, seq)
    ro, s_ret = _ret_prompt(qr, kr, p, rc, bsz, seq)
    go, s_gla = _gla_prompt(p, la, gc, bsz, seq)
    yc, s_lru = _lru_prompt(p3, w["lru"], bsz, seq)
    s_conv = p3[:, seq - (CONV_W - 1):, COL_LX:COL_LX + MIX]
    out = _token_tail(x, p, ys, ro, yc, go, w, gfin, final_norm, tm)
    return out, (s5_re, s5_im, s_ret, s_lru, s_conv, s_gla)


def _layer_sample(x, st, w, rc, rope, gfin, final_norm):
    t = x.shape[0]
    s5_re0, s5_im0, ret0, lru0, conv0, gla0 = st
    p, ga = _inproj(x, w["norm_mix"], w["w_main"], w["w_ga"], t)
    qr, kr, la = _prep(p, ga, rope[0], rope[1], w["wa2"], w["ba2"], t)
    ys, s5_re, s5_im = _s5_step(p, s5_re0.reshape(t, -1), s5_im0.reshape(t, -1), w["s5"])
    ro, s_ret, go, s_gla = _mat_step(qr, kr, la, p, ret0, gla0, rc["rdec"])
    c0 = conv0.transpose(1, 0, 2).reshape((CONV_W - 1) * t, MIX)
    yc, s_lru, cf = _lru(p[:, COL_LX:COL_LX + MIX], p[:, COL_LG:COL_LG + MIX], lru0, c0, w["lru"], t, 1)
    s_conv = cf.reshape(CONV_W - 1, t, MIX).transpose(1, 0, 2)
    out = _token_tail(x, p, ys, ro, yc, go, w, gfin, final_norm, t)
    return out, (s5_re.reshape(s5_re0.shape), s5_im.reshape(s5_im0.shape), s_ret, s_lru, s_conv, s_gla)


_PARAM_NAMES = ("norm_mix", "w_in", "s5_lambda_re", "s5_lambda_im", "s5_log_dt", "s5_b_re", "s5_b_im", "s5_c_re",
                "s5_c_im", "s5_d", "s5_w_glu", "s5_b_glu", "ret_norm", "lru_conv_w", "lru_conv_b", "lru_w_a",
                "lru_b_a", "lru_w_x", "lru_b_x", "lru_lambda", "gla_w_a2", "gla_b_a2", "gla_norm", "w_branch",
                "w_out", "norm_ffn", "peer_w_q", "peer_sub_keys", "peer_u", "peer_v")


def _forward(x_prompt, x_sample, states, params, norm_final, tm):
    bsz, seq, _ = x_prompt.shape
    nsmp = x_sample.shape[0]
    depth = params["w_in"].shape[0]
    rc = _ret_consts()
    gc = _gla_consts()
    rope_p = _rope_tables(jnp.arange(seq, dtype=jnp.int32))
    rope_s = _rope_tables(jnp.full((nsmp,), PAST_LEN, dtype=jnp.int32))
    gfin = norm_final.reshape(1, -1)
    hp = x_prompt.reshape(bsz * seq, D_MODEL)
    hs = x_sample.reshape(nsmp, D_MODEL)
    new_p, new_s = [], []
    for l in range(depth):
        w = _layer_weights({k: v[l] for k, v in params.items()})
        last = l == depth - 1
        hp, sp = _layer_prompt(hp, w, rc, gc, rope_p, gfin, last, bsz, seq, tm)
        hs, ss = _layer_sample(hs, tuple(s[l] for s in states), w, rc, rope_s, gfin, last)
        new_p.append(sp)
        new_s.append(ss)
    outs = [hp.reshape(bsz, seq, D_MODEL), hs.reshape(nsmp, 1, D_MODEL)]
    outs += [jnp.stack([s[i] for s in new_p]) for i in range(6)]
    outs += [jnp.stack([s[i] for s in new_s]) for i in range(6)]
    return tuple(outs)


def kernel(x_prompt, x_sample, state_s5_re, state_s5_im, state_ret, state_lru, state_conv, state_gla, norm_mix, w_in, s5_lambda_re, s5_lambda_im, s5_log_dt, s5_b_re, s5_b_im, s5_c_re, s5_c_im, s5_d, s5_w_glu, s5_b_glu, ret_norm, lru_conv_w, lru_conv_b, lru_w_a, lru_b_a, lru_w_x, lru_b_x, lru_lambda, gla_w_a2, gla_b_a2, gla_norm, w_branch, w_out, norm_ffn, peer_w_q, peer_sub_keys, peer_u, peer_v, norm_final):
    values = (norm_mix, w_in, s5_lambda_re, s5_lambda_im, s5_log_dt, s5_b_re, s5_b_im, s5_c_re, s5_c_im, s5_d,
              s5_w_glu, s5_b_glu, ret_norm, lru_conv_w, lru_conv_b, lru_w_a, lru_b_a, lru_w_x, lru_b_x, lru_lambda,
              gla_w_a2, gla_b_a2, gla_norm, w_branch, w_out, norm_ffn, peer_w_q, peer_sub_keys, peer_u, peer_v)
    params = dict(zip(_PARAM_NAMES, values))
    states = (state_s5_re, state_s5_im, state_ret, state_lru, state_conv, state_gla)
    return _forward(x_prompt, x_sample, states, params, norm_final, 256)
```

```python
import functools
import math

import jax
import jax.numpy as jnp
from jax import lax
from jax.experimental import pallas as pl
from jax.experimental.pallas import tpu as pltpu

F32 = jnp.float32
BF16 = jnp.bfloat16

D_MODEL = 1024
MIX = 512
NORM_EPS = 1e-6
PAST_LEN = 16384
S5_GROUPS = 32
S5_GROUP = 16
S5_STATE = 64
HEADS = 4
DK = 64
DV = 128
QK = HEADS * DK
ROPE_BASE = 10000.0
RET_CHUNK = 256
GLA_CHUNK = 64
GLA_BLOCK = 256
GLA_RANK = 16
GLA_TAU = 16.0
LRU_BLOCKS = 8
LRU_C = 8.0
CONV_W = 4
PEER_HEADS = 8
PEER_NK = 128
PEER_EXPERTS = PEER_NK * PEER_NK
PEER_TOPK = 16
PEER_EBLK = 2048

COL_MG = 0
COL_U = 4096
COL_RQ = 4608
COL_RK = 4864
COL_RV = 5120
COL_RG = 5632
COL_LX = 6144
COL_LG = 6656
COL_GQ = 7168
COL_GK = 7424
COL_GV = 7680
COL_GG = 8192
IN_MAIN = 8704
IN_TILE = IN_MAIN // 4
INPROJ_ROWS = 1024

VMEM_LIMIT = 52 * 1024 * 1024


def _cparams(sem, flags=None):
    return pltpu.CompilerParams(dimension_semantics=sem, vmem_limit_bytes=VMEM_LIMIT, flags=flags)


def _dot(a, b):
    return jnp.dot(a, b, preferred_element_type=F32)


def _dot_nt(a, b):
    return lax.dot_general(a, b, (((1,), (1,)), ((), ())), preferred_element_type=F32)


def _dot_tn(a, b):
    return lax.dot_general(a, b, (((0,), (0,)), ((), ())), preferred_element_type=F32)


def _sigmoid(x):
    return 1.0 / (1.0 + jnp.exp(-x))


def _silu(x):
    return x * _sigmoid(x)


_GELU_C0 = -2.0 * math.sqrt(2.0 / math.pi) * math.log2(math.e)
_GELU_C1 = 0.044715 * _GELU_C0


def _gelu(x):
    return x / (1.0 + jnp.exp2(x * (_GELU_C1 * (x * x) + _GELU_C0)))


def _softplus(x):
    return jnp.maximum(x, 0.0) + jnp.log1p(jnp.exp(-jnp.abs(x)))


def _rms(x, g):
    return x * lax.rsqrt(jnp.mean(x * x, axis=-1, keepdims=True) + NORM_EPS) * g


def _inproj_kernel(x_ref, g_ref, w_ref, wga_ref, o_ref, ga_ref, n_ref):
    @pl.when(pl.program_id(1) == 0)
    def _():
        n = _rms(x_ref[...], g_ref[...]).astype(BF16)
        n_ref[...] = n
        ga_ref[...] = _dot(n, wga_ref[...])

    o_ref[...] = _dot(n_ref[...], w_ref[...])


def _inproj(x, g, w, wga, tm):
    t = x.shape[0]
    return pl.pallas_call(
        _inproj_kernel,
        out_shape=(jax.ShapeDtypeStruct((t, IN_MAIN), F32), jax.ShapeDtypeStruct((t, 128), F32)),
        grid=(t // tm, IN_MAIN // IN_TILE),
        in_specs=[pl.BlockSpec((tm, D_MODEL), lambda i, j: (i, 0)),
                  pl.BlockSpec((1, D_MODEL), lambda i, j: (0, 0)),
                  pl.BlockSpec((D_MODEL, IN_TILE), lambda i, j: (0, j)),
                  pl.BlockSpec((D_MODEL, 128), lambda i, j: (0, 0))],
        out_specs=(pl.BlockSpec((tm, IN_TILE), lambda i, j: (i, j)),
                   pl.BlockSpec((tm, 128), lambda i, j: (i, 0))),
        scratch_shapes=[pltpu.VMEM((tm, D_MODEL), BF16)],
        compiler_params=_cparams(("parallel", "arbitrary")),
        name="inproj",
    )(x, g, w, wga)


def _prep_kernel(rq_ref, rk_ref, ga_ref, cos_ref, sin_ref, wa2_ref, ba2_ref, qr_ref, kr_ref, la_ref):
    cos = cos_ref[...]
    sin = sin_ref[...]
    q = rq_ref[...]
    k = rk_ref[...]
    q1, q2 = q[:, :128], q[:, 128:]
    k1, k2 = k[:, :128], k[:, 128:]
    qr_ref[:, :128] = q1 * cos - q2 * sin
    qr_ref[:, 128:] = q1 * sin + q2 * cos
    kr_ref[:, :128] = (k1 * cos - k2 * sin) * (DK ** -0.5)
    kr_ref[:, 128:] = (k1 * sin + k2 * cos) * (DK ** -0.5)
    z = _dot(ga_ref[...].astype(BF16), wa2_ref[...]) + ba2_ref[...]
    la_ref[...] = (jnp.minimum(z, 0.0) - jnp.log1p(jnp.exp(-jnp.abs(z)))) * (1.0 / GLA_TAU)


def _prep(p, ga, cos, sin, wa2, ba2, tm):
    t = p.shape[0]
    n_pos = cos.shape[0] // tm
    return pl.pallas_call(
        _prep_kernel,
        out_shape=(jax.ShapeDtypeStruct((t, QK), F32),) * 3,
        grid=(t // tm,),
        in_specs=[pl.BlockSpec((tm, QK), lambda i: (i, COL_RQ // QK)),
                  pl.BlockSpec((tm, QK), lambda i: (i, COL_RK // QK)),
                  pl.BlockSpec((tm, 128), lambda i: (i, 0)),
                  pl.BlockSpec((tm, 128), lambda i: (i % n_pos, 0)),
                  pl.BlockSpec((tm, 128), lambda i: (i % n_pos, 0)),
                  pl.BlockSpec((128, QK), lambda i: (0, 0)),
                  pl.BlockSpec((1, QK), lambda i: (0, 0))],
        out_specs=(pl.BlockSpec((tm, QK), lambda i: (i, 0)),) * 3,
        compiler_params=_cparams(("parallel",)),
        name="prep",
    )(p, p, ga, cos, sin, wa2, ba2)


S5_SLABS = S5_GROUPS * S5_STATE // 128
S5_LANE_BLOCKS = MIX // 128
S5_SLABS_PER_BLOCK = S5_SLABS // S5_LANE_BLOCKS
S5_SCAN_SLABS = 4
SCAN_STEPS = 64


def _s5_prompt_kernel(u_ref, perm_ref, bre_ref, bim_ref, abr_ref, abi_ref, cre_ref, cim_ref,
                      y_ref, sr_ref, si_ref, wre, wim, st_re, st_im, *, tb, bsz):
    rows = bsz * tb
    spb = S5_SLABS_PER_BLOCK * 128

    @pl.when(pl.program_id(0) == 0)
    def _():
        st_re[...] = jnp.zeros_like(st_re)
        st_im[...] = jnp.zeros_like(st_im)

    perm = perm_ref[...]
    u = _dot(perm, u_ref[...].reshape(rows, MIX).astype(BF16)).astype(BF16)
    for lb in range(S5_LANE_BLOCKS):
        ul = u[:, lb * 128:(lb + 1) * 128]
        wre[:, lb * spb:(lb + 1) * spb] = _dot(ul, bre_ref[lb])
        wim[:, lb * spb:(lb + 1) * spb] = _dot(ul, bim_ref[lb])

    width = S5_SCAN_SLABS * 128
    for s0 in range(0, S5_SLABS * 128, width):
        lanes = slice(s0, s0 + width)
        a_re = jnp.broadcast_to(abr_ref[:, lanes], (bsz, width))
        a_im = jnp.broadcast_to(abi_ref[:, lanes], (bsz, width))

        def body(t, carry, lanes=lanes, a_re=a_re, a_im=a_im):
            s_re, s_im = carry
            sl = pl.ds(pl.multiple_of(t * bsz, bsz), bsz)
            n_re = a_re * s_re - a_im * s_im + wre[sl, lanes]
            n_im = a_re * s_im + a_im * s_re + wim[sl, lanes]
            wre[sl, lanes] = n_re
            wim[sl, lanes] = n_im
            return n_re, n_im

        s_re, s_im = lax.fori_loop(0, tb, body, (st_re[:, lanes], st_im[:, lanes]))
        st_re[:, lanes] = s_re
        st_im[:, lanes] = s_im
    sr_ref[...] = st_re[...]
    si_ref[...] = st_im[...]

    ys = []
    for lb in range(S5_LANE_BLOCKS):
        lanes = slice(lb * spb, (lb + 1) * spb)
        s_re = _dot_tn(perm, wre[:, lanes].astype(BF16)).astype(BF16)
        s_im = _dot_tn(perm, wim[:, lanes].astype(BF16)).astype(BF16)
        ys.append(_dot(s_re, cre_ref[lb]) - _dot(s_im, cim_ref[lb]))
    y_ref[...] = jnp.concatenate(ys, axis=-1).reshape(bsz, tb, MIX)


def _s5_prompt(p3, w, bsz, seq):
    tb = min(seq, SCAN_STEPS)
    rows = bsz * tb
    ns = S5_GROUPS * S5_STATE
    spb = S5_SLABS_PER_BLOCK * 128
    r = jnp.arange(rows)
    perm = ((r % bsz) * tb + r // bsz)[:, None] == r[None, :]
    c3 = lambda a, b, c: pl.BlockSpec((a, b, c), lambda i: (0, 0, 0))
    c2 = lambda a, b: pl.BlockSpec((a, b), lambda i: (0, 0))
    y, s_re, s_im = pl.pallas_call(
        functools.partial(_s5_prompt_kernel, tb=tb, bsz=bsz),
        out_shape=(jax.ShapeDtypeStruct((bsz, seq, MIX), F32),
                   jax.ShapeDtypeStruct((bsz, ns), F32), jax.ShapeDtypeStruct((bsz, ns), F32)),
        grid=(seq // tb,),
        in_specs=[pl.BlockSpec((bsz, tb, MIX), lambda i: (0, i, COL_U // MIX)), c2(rows, rows),
                  c3(S5_LANE_BLOCKS, 128, spb), c3(S5_LANE_BLOCKS, 128, spb), c2(1, ns), c2(1, ns),
                  c3(S5_LANE_BLOCKS, spb, 128), c3(S5_LANE_BLOCKS, spb, 128)],
        out_specs=(pl.BlockSpec((bsz, tb, MIX), lambda i: (0, i, 0)), c2(bsz, ns), c2(bsz, ns)),
        scratch_shapes=[pltpu.VMEM((rows, ns), F32), pltpu.VMEM((rows, ns), F32),
                        pltpu.VMEM((bsz, ns), F32), pltpu.VMEM((bsz, ns), F32)],
        compiler_params=_cparams(("arbitrary",)),
        name="s5_prompt",
    )(p3, perm.astype(BF16), w["bc_re"], w["bc_im"], w["ab_re"], w["ab_im"], w["cc_re"], w["cc_im"])
    to_state = lambda s: s.reshape(bsz, S5_GROUPS, S5_STATE)
    return y.reshape(bsz * seq, MIX), to_state(s_re), to_state(s_im)


def _s5_step_kernel(u_ref, hr_ref, hi_ref, bre_ref, bim_ref, abr_ref, abi_ref, cre_ref, cim_ref,
                    y_ref, sr_ref, si_ref):
    u = u_ref[...].astype(BF16)
    h_re = hr_ref[...]
    h_im = hi_ref[...]
    ab_re = abr_ref[...]
    ab_im = abi_ref[...]
    s_re = _dot(u, bre_ref[...]) + (ab_re * h_re - ab_im * h_im)
    s_im = _dot(u, bim_ref[...]) + (ab_re * h_im + ab_im * h_re)
    sr_ref[...] = s_re
    si_ref[...] = s_im
    y_ref[...] = _dot(s_re.astype(BF16), cre_ref[...]) - _dot(s_im.astype(BF16), cim_ref[...])


def _s5_step(p, h_re, h_im, w):
    t = p.shape[0]
    ns = S5_GROUPS * S5_STATE
    full = lambda a, b: pl.BlockSpec((a, b), lambda i: (0, 0))
    return pl.pallas_call(
        _s5_step_kernel,
        out_shape=(jax.ShapeDtypeStruct((t, MIX), F32), jax.ShapeDtypeStruct((t, ns), F32),
                   jax.ShapeDtypeStruct((t, ns), F32)),
        grid=(1,),
        in_specs=[pl.BlockSpec((t, MIX), lambda i: (0, COL_U // MIX)), full(t, ns), full(t, ns),
                  full(MIX, ns), full(MIX, ns), full(1, ns), full(1, ns), full(ns, MIX), full(ns, MIX)],
        out_specs=(full(t, MIX), full(t, ns), full(t, ns)),
        compiler_params=_cparams(("arbitrary",)),
        name="s5_step",
    )(p, h_re, h_im, w["bd_re"], w["bd_im"], w["ab_re"], w["ab_im"], w["cd_re"], w["cd_im"])


def _lru_coeffs(xc, wa_ref, ba_ref, wx_ref, bx_ref, lam_ref):
    xcb = xc.astype(BF16)
    r = _sigmoid(_dot(xcb, wa_ref[...]) + ba_ref[...])
    i = _sigmoid(_dot(xcb, wx_ref[...]) + bx_ref[...])
    log_a = (-LRU_C) * r * _softplus(-lam_ref[...])
    a = jnp.exp(log_a)
    return a, jnp.sqrt(-jnp.tanh(log_a) * (a * a + 1.0)) * (i * xc)


LRU_SLABS = MIX // 128
CONV_HALO = 8


def _lru_prompt_kernel(x_ref, g_ref, cw_ref, cb_ref, wa_ref, ba_ref, wx_ref, bx_ref, lam_ref,
                       y_ref, hf_ref, xbuf, hs, abuf, bbuf, *, tb, bsz):
    rows = bsz * tb

    @pl.when(pl.program_id(0) == 0)
    def _():
        xbuf[:, 0:CONV_HALO, :] = jnp.zeros((bsz, CONV_HALO, MIX), F32)
        hs[...] = jnp.zeros_like(hs)

    xbuf[:, CONV_HALO:CONV_HALO + tb, :] = x_ref[...]
    first = CONV_HALO - (CONV_W - 1)
    xc = cb_ref[...] + xbuf[:, first:first + tb, :] * cw_ref[0:1, :]
    for j in range(1, CONV_W):
        xc = xc + xbuf[:, first + j:first + j + tb, :] * cw_ref[j:j + 1, :]
    a, b = _lru_coeffs(xc.reshape(rows, MIX), wa_ref, ba_ref, wx_ref, bx_ref, lam_ref)
    for l in range(LRU_SLABS):
        abuf[l] = a[:, l * 128:(l + 1) * 128]
        bbuf[l] = b[:, l * 128:(l + 1) * 128]

    def body(t, hcar):
        sl = pl.ds(t, bsz, stride=tb)
        out = []
        for l in range(LRU_SLABS):
            h = abuf[l, sl, :] * hcar[l] + bbuf[l, sl, :]
            bbuf[l, sl, :] = h
            out.append(h)
        return tuple(out)

    fin = lax.fori_loop(0, tb, body, tuple(hs[l] for l in range(LRU_SLABS)))
    for l in range(LRU_SLABS):
        hs[l] = fin[l]
    hf_ref[...] = hs[...]
    h_all = jnp.concatenate([bbuf[l] for l in range(LRU_SLABS)], axis=-1)
    y_ref[...] = (h_all * _gelu(g_ref[...].reshape(rows, MIX))).reshape(bsz, tb, MIX)
    xbuf[:, first:CONV_HALO, :] = xbuf[:, first + tb:CONV_HALO + tb, :]


def _lru_prompt(p3, w, bsz, seq):
    tb = min(seq, SCAN_STEPS)
    full = lambda a, b: pl.BlockSpec((a, b), lambda i: (0, 0))
    y, hf = pl.pallas_call(
        functools.partial(_lru_prompt_kernel, tb=tb, bsz=bsz),
        out_shape=(jax.ShapeDtypeStruct((bsz, seq, MIX), F32), jax.ShapeDtypeStruct((LRU_SLABS, bsz, 128), F32)),
        grid=(seq // tb,),
        in_specs=[pl.BlockSpec((bsz, tb, MIX), lambda i: (0, i, COL_LX // MIX)),
                  pl.BlockSpec((bsz, tb, MIX), lambda i: (0, i, COL_LG // MIX)),
                  full(CONV_W, MIX), full(1, MIX), full(MIX, MIX), full(1, MIX), full(MIX, MIX), full(1, MIX),
                  full(1, MIX)],
        out_specs=(pl.BlockSpec((bsz, tb, MIX), lambda i: (0, i, 0)),
                   pl.BlockSpec((LRU_SLABS, bsz, 128), lambda i: (0, 0, 0))),
        scratch_shapes=[pltpu.VMEM((bsz, CONV_HALO + tb, MIX), F32), pltpu.VMEM((LRU_SLABS, bsz, 128), F32),
                        pltpu.VMEM((LRU_SLABS, bsz * tb, 128), F32), pltpu.VMEM((LRU_SLABS, bsz * tb, 128), F32)],
        compiler_params=_cparams(("arbitrary",)),
        name="rglru_prompt",
    )(p3, p3, w["conv_w"], w["conv_b"], w["wa"], w["ba"], w["wx"], w["bx"], w["lam"])
    return y.reshape(bsz * seq, MIX), hf.transpose(1, 0, 2).reshape(bsz, MIX)


def _lru_kernel(x_ref, g_ref, h0_ref, c0_ref, cw_ref, cb_ref, wa_ref, ba_ref, wx_ref, bx_ref, lam_ref,
                y_ref, hf_ref, cf_ref, xbuf, hs, abuf, bbuf, *, steps, bsz):
    rows = steps * bsz
    halo = (CONV_W - 1) * bsz

    @pl.when(pl.program_id(0) == 0)
    def _():
        xbuf[0:halo, :] = c0_ref[...]
        hs[...] = h0_ref[...]

    xbuf[halo:halo + rows, :] = x_ref[...]
    xc = cb_ref[...] + xbuf[0:rows, :] * cw_ref[0:1, :]
    for j in range(1, CONV_W):
        xc = xc + xbuf[j * bsz:j * bsz + rows, :] * cw_ref[j:j + 1, :]
    a, b = _lru_coeffs(xc, wa_ref, ba_ref, wx_ref, bx_ref, lam_ref)
    abuf[...] = a
    bbuf[...] = b

    def body(t, h):
        sl = pl.ds(pl.multiple_of(t * bsz, bsz), bsz)
        h = abuf[sl, :] * h + bbuf[sl, :]
        bbuf[sl, :] = h
        return h

    h = lax.fori_loop(0, steps, body, hs[...])
    hs[...] = h
    hf_ref[...] = h
    y_ref[...] = bbuf[...] * _gelu(g_ref[...])
    tail = xbuf[rows:rows + halo, :]
    xbuf[0:halo, :] = tail
    cf_ref[...] = tail


def _lru(x, g, h0, c0, w, bsz, steps):
    t = x.shape[0]
    rows = steps * bsz
    halo = (CONV_W - 1) * bsz
    full = lambda a, b: pl.BlockSpec((a, b), lambda i: (0, 0))
    return pl.pallas_call(
        functools.partial(_lru_kernel, steps=steps, bsz=bsz),
        out_shape=(jax.ShapeDtypeStruct((t, MIX), F32), jax.ShapeDtypeStruct((bsz, MIX), F32),
                   jax.ShapeDtypeStruct((halo, MIX), F32)),
        grid=(t // rows,),
        in_specs=[pl.BlockSpec((rows, MIX), lambda i: (i, 0)), pl.BlockSpec((rows, MIX), lambda i: (i, 0)),
                  full(bsz, MIX), full(halo, MIX), full(CONV_W, MIX), full(1, MIX),
                  full(MIX, MIX), full(1, MIX), full(MIX, MIX), full(1, MIX), full(1, MIX)],
        out_specs=(pl.BlockSpec((rows, MIX), lambda i: (i, 0)), full(bsz, MIX), full(halo, MIX)),
        scratch_shapes=[pltpu.VMEM((rows + halo, MIX), F32), pltpu.VMEM((bsz, MIX), F32),
                        pltpu.VMEM((rows, MIX), F32), pltpu.VMEM((rows, MIX), F32)],
        compiler_params=_cparams(("arbitrary",)),
        name="rglru",
    )(x, g, h0, c0, w["conv_w"], w["conv_b"], w["wa"], w["ba"], w["wx"], w["bx"], w["lam"])


def _ret_prompt_kernel(q_ref, k_ref, v_ref, dec_ref, wq_ref, wk_ref, gc_ref, hm_ref, o_ref, sf_ref, s_ref):
    @pl.when(pl.program_id(1) == 0)
    def _():
        s_ref[...] = jnp.zeros_like(s_ref)

    q = q_ref[...]
    k = k_ref[...]
    kb = k.astype(BF16)
    vb = v_ref[...].astype(BF16)
    sb = s_ref[...].astype(BF16)
    qw = q * wq_ref[...]
    for h in range(HEADS):
        hm = hm_ref[h]
        vs = slice(h * DV, (h + 1) * DV)
        sc = _dot_nt((q * hm).astype(BF16), kb) * dec_ref[h]
        o_ref[:, vs] = _dot(sc.astype(BF16), vb[:, vs]) + _dot((qw * hm).astype(BF16), sb[:, vs])
    s_new = s_ref[...] * gc_ref[...] + _dot_tn((k * wk_ref[...]).astype(BF16), vb)
    s_ref[...] = s_new
    sf_ref[0] = s_new


def _ret_prompt(qr, kr, p, c, bsz, seq):
    ch = RET_CHUNK
    n = seq // ch
    full2 = lambda a, b: pl.BlockSpec((a, b), lambda bi, ci: (0, 0))
    o, sf = pl.pallas_call(
        _ret_prompt_kernel,
        out_shape=(jax.ShapeDtypeStruct((bsz * seq, MIX), F32), jax.ShapeDtypeStruct((bsz, QK, MIX), F32)),
        grid=(bsz, n),
        in_specs=[pl.BlockSpec((ch, QK), lambda bi, ci: (bi * n + ci, 0)),
                  pl.BlockSpec((ch, QK), lambda bi, ci: (bi * n + ci, 0)),
                  pl.BlockSpec((ch, MIX), lambda bi, ci: (bi * n + ci, COL_RV // MIX)),
                  pl.BlockSpec((HEADS, ch, ch), lambda bi, ci: (0, 0, 0)),
                  full2(ch, QK), full2(ch, QK), full2(1, MIX),
                  pl.BlockSpec((HEADS, 1, QK), lambda bi, ci: (0, 0, 0))],
        out_specs=(pl.BlockSpec((ch, MIX), lambda bi, ci: (bi * n + ci, 0)),
                   pl.BlockSpec((1, QK, MIX), lambda bi, ci: (bi, 0, 0))),
        scratch_shapes=[pltpu.VMEM((QK, MIX), F32)],
        compiler_params=_cparams(("parallel", "arbitrary")),
        name="ret_prompt",
    )(qr, kr, p, c["dec"], c["wq"], c["wk"], c["gc"], c["hmask"])
    s5d = sf.reshape(bsz, 2, HEADS, 32, HEADS, DV)
    st = jnp.stack([s5d[:, :, h, :, h, :] for h in range(HEADS)], axis=1)
    return o, st.reshape(bsz, HEADS, DK, DV)


def _gla_prompt_kernel(q_ref, k_ref, v_ref, la_ref, tril_ref, hm_ref, o_ref, sf_ref, s_ref):
    @pl.when(pl.program_id(1) == 0)
    def _():
        s_ref[...] = jnp.zeros_like(s_ref)

    tril = tril_ref[...]
    for c in range(GLA_BLOCK // GLA_CHUNK):
        rs = slice(c * GLA_CHUNK, (c + 1) * GLA_CHUNK)
        b = jnp.dot(tril, la_ref[rs, :], preferred_element_type=F32, precision=lax.Precision.HIGHEST)
        b_last = b[GLA_CHUNK - 1:GLA_CHUNK, :]
        q = q_ref[rs, :] * (DK ** -0.5)
        k = k_ref[rs, :]
        q_t = q * jnp.exp(b)
        k_t = (k * jnp.exp(-b)).astype(BF16)
        k_s = (k * jnp.exp(b_last - b)).astype(BF16)
        vb = v_ref[rs, :].astype(BF16)
        sb = s_ref[...].astype(BF16)
        for h in range(HEADS):
            qm = (q_t * hm_ref[h]).astype(BF16)
            vs = slice(h * DV, (h + 1) * DV)
            sc = jnp.where(tril > 0.0, _dot_nt(qm, k_t), 0.0)
            o_ref[rs, vs] = _dot(sc.astype(BF16), vb[:, vs]) + _dot_nt(qm, sb[vs, :])
        s_ref[...] = s_ref[...] * jnp.exp(b_last) + _dot_tn(vb, k_s)
    sf_ref[0] = s_ref[...]


def _gla_prompt(p, la, c, bsz, seq):
    blk = GLA_BLOCK
    n = seq // blk
    o, sf = pl.pallas_call(
        _gla_prompt_kernel,
        out_shape=(jax.ShapeDtypeStruct((bsz * seq, MIX), F32), jax.ShapeDtypeStruct((bsz, MIX, QK), F32)),
        grid=(bsz, n),
        in_specs=[pl.BlockSpec((blk, QK), lambda bi, ci: (bi * n + ci, COL_GQ // QK)),
                  pl.BlockSpec((blk, QK), lambda bi, ci: (bi * n + ci, COL_GK // QK)),
                  pl.BlockSpec((blk, MIX), lambda bi, ci: (bi * n + ci, COL_GV // MIX)),
                  pl.BlockSpec((blk, QK), lambda bi, ci: (bi * n + ci, 0)),
                  pl.BlockSpec((GLA_CHUNK, GLA_CHUNK), lambda bi, ci: (0, 0)),
                  pl.BlockSpec((HEADS, 1, QK), lambda bi, ci: (0, 0, 0))],
        out_specs=(pl.BlockSpec((blk, MIX), lambda bi, ci: (bi * n + ci, 0)),
                   pl.BlockSpec((1, MIX, QK), lambda bi, ci: (bi, 0, 0))),
        scratch_shapes=[pltpu.VMEM((MIX, QK), F32)],
        compiler_params=_cparams(("parallel", "arbitrary")),
        name="gla_prompt",
    )(p, p, p, la, c["tril"], c["hmask"])
    s5d = sf.reshape(bsz, HEADS, DV, HEADS, DK)
    st = jnp.stack([s5d[:, h, :, h, :] for h in range(HEADS)], axis=1)
    return o, st.transpose(0, 1, 3, 2)


STEP_GROUP = 8


def _mat_step_kernel(rq_ref, rk_ref, rv_ref, rs_ref, gq_ref, gk_ref, ga_ref, gv_ref, gs_ref, rdec_ref,
                     ro_ref, rso_ref, go_ref, gso_ref):
    rq = rq_ref[0]
    rk = rk_ref[0]
    gq = gq_ref[0] * (DK ** -0.5)
    gk = gk_ref[0]
    ga = jnp.exp(ga_ref[0])
    for i in range(STEP_GROUP):
        for h in range(HEADS):
            ks = slice(h * DK, (h + 1) * DK)
            vs = slice(h * DV, (h + 1) * DV)
            s1 = rs_ref[i, h] * rdec_ref[h] + rk[ks, i:i + 1] * rv_ref[i:i + 1, vs]
            rso_ref[i, h] = s1
            ro_ref[i:i + 1, vs] = jnp.sum(rq[ks, i:i + 1] * s1, axis=0, keepdims=True)
            s1 = gs_ref[i, h] * ga[ks, i:i + 1] + gk[ks, i:i + 1] * gv_ref[i:i + 1, vs]
            gso_ref[i, h] = s1
            go_ref[i:i + 1, vs] = jnp.sum(gq[ks, i:i + 1] * s1, axis=0, keepdims=True)


def _mat_step(qr, kr, la, p, s_ret, s_gla, layer, rdec):
    t = p.shape[0]
    g = STEP_GROUP
    ng = t // g

    def tr(x):
        return x.reshape(ng, g, QK).transpose(0, 2, 1)

    def unsplit(x):
        return x.reshape(t, 2, HEADS, 32).transpose(0, 2, 1, 3).reshape(t, QK)

    gq = p[:, COL_GQ:COL_GQ + QK]
    gk = p[:, COL_GK:COL_GK + QK]
    tspec = pl.BlockSpec((1, QK, g), lambda i: (i, 0, 0))
    sspec = pl.BlockSpec((g, HEADS, DK, DV), lambda i: (i, 0, 0, 0))
    sspec_in = pl.BlockSpec((None, g, HEADS, DK, DV), lambda i: (layer, i, 0, 0, 0))
    vspec = lambda col: pl.BlockSpec((g, MIX), lambda i: (i, col // MIX))
    ospec = pl.BlockSpec((g, MIX), lambda i: (i, 0))
    return pl.pallas_call(
        _mat_step_kernel,
        out_shape=(jax.ShapeDtypeStruct((t, MIX), F32), jax.ShapeDtypeStruct(s_ret.shape[1:], F32),
                   jax.ShapeDtypeStruct((t, MIX), F32), jax.ShapeDtypeStruct(s_gla.shape[1:], F32)),
        grid=(ng,),
        in_specs=[tspec, tspec, vspec(COL_RV), sspec_in, tspec, tspec, tspec, vspec(COL_GV), sspec_in,
                  pl.BlockSpec((HEADS, 1, DV), lambda i: (0, 0, 0))],
        out_specs=(ospec, sspec, ospec, sspec),
        compiler_params=_cparams(("parallel",)),
        name="mat_step",
    )(tr(unsplit(qr)), tr(unsplit(kr)), p, s_ret, tr(gq), tr(gk), tr(la), p, s_gla, rdec)


def _head_norm(o, g):
    parts = []
    for h in range(HEADS):
        oh = o[:, h * DV:(h + 1) * DV]
        mu = jnp.mean(oh, axis=-1, keepdims=True)
        d = oh - mu
        var = jnp.mean(d * d, axis=-1, keepdims=True)
        parts.append(d * lax.rsqrt(var + NORM_EPS))
    return jnp.concatenate(parts, axis=-1) * g


def _merge_kernel(x_ref, mg_ref, u_ref, rg_ref, gg_ref, ys_ref, ro_ref, yc_ref, go_ref,
                  d_ref, wglu_ref, bglu_ref, gret_ref, ggla_ref, wb_ref, wout_ref, gffn_ref,
                  h_ref, n_ref):
    ya = _gelu(ys_ref[...] + d_ref[...] * u_ref[...])
    ya = ya * _sigmoid(_dot(ya.astype(BF16), wglu_ref[...]) + bglu_ref[...])
    yb = _silu(rg_ref[...]) * _head_norm(ro_ref[...], gret_ref[...])
    yc = yc_ref[...]
    yd = _silu(gg_ref[...]) * _head_norm(go_ref[...], ggla_ref[...])
    m = None
    for i, y in enumerate((ya, yb, yc, yd)):
        z = _dot(y.astype(BF16), wb_ref[i])
        gz = _sigmoid(mg_ref[:, i * D_MODEL:(i + 1) * D_MODEL]) * z
        m = gz if m is None else m + gz
    h = x_ref[...] + _dot(m.astype(BF16), wout_ref[...])
    h_ref[...] = h
    n_ref[...] = _rms(h, gffn_ref[...]).astype(BF16)


def _merge(x, p, ys, ro, yc, go, w, tm):
    t = x.shape[0]
    row = lambda width, col=0: pl.BlockSpec((tm, width), lambda i: (i, col // width))
    full = lambda a, b: pl.BlockSpec((a, b), lambda i: (0, 0))
    return pl.pallas_call(
        _merge_kernel,
        out_shape=(jax.ShapeDtypeStruct((t, D_MODEL), F32), jax.ShapeDtypeStruct((t, D_MODEL), BF16)),
        grid=(t // tm,),
        in_specs=[row(D_MODEL), row(4 * D_MODEL, COL_MG), row(MIX, COL_U), row(MIX, COL_RG), row(MIX, COL_GG),
                  row(MIX), row(MIX), row(MIX), row(MIX),
                  full(1, MIX), full(MIX, MIX), full(1, MIX), full(1, MIX), full(1, MIX),
                  pl.BlockSpec((4, MIX, D_MODEL), lambda i: (0, 0, 0)), full(D_MODEL, D_MODEL), full(1, D_MODEL)],
        out_specs=(row(D_MODEL), row(D_MODEL)),
        compiler_params=_cparams(("parallel",)),
        name="merge",
    )(x, p, p, p, p, ys, ro, yc, go, w["s5_d"], w["w_glu"], w["b_glu"], w["ret_norm"], w["gla_norm"],
      w["w_branch"], w["w_out"], w["norm_ffn"])


def _scores_kernel(n_ref, wq_ref, sk_ref, st_ref):
    q = _dot(n_ref[...], wq_ref[...]).astype(BF16)
    for hp in range(2 * PEER_HEADS):
        st_ref[hp] = _dot_nt(sk_ref[hp], q[:, hp * PEER_NK:(hp + 1) * PEER_NK])


def _scores(n2, wq, sk, tm):
    t = n2.shape[0]
    return pl.pallas_call(
        _scores_kernel,
        out_shape=jax.ShapeDtypeStruct((2 * PEER_HEADS, PEER_NK, t), F32),
        grid=(t // tm,),
        in_specs=[pl.BlockSpec((tm, D_MODEL), lambda i: (i, 0)),
                  pl.BlockSpec((D_MODEL, 2 * PEER_HEADS * PEER_NK), lambda i: (0, 0)),
                  pl.BlockSpec((2 * PEER_HEADS, PEER_NK, PEER_NK), lambda i: (0, 0, 0))],
        out_specs=pl.BlockSpec((2 * PEER_HEADS, PEER_NK, tm), lambda i: (0, 0, i)),
        compiler_params=_cparams(("parallel",)),
        name="peer_scores",
    )(n2, wq, sk)


PEER_NTOP = PEER_TOPK + 1
PEER_CAND_SPANS = tuple(PEER_NTOP // (a + 1) for a in range(PEER_NTOP))
PEER_NCAND = sum(PEER_CAND_SPANS)
PEER_CAND_ROWS = -(-PEER_NCAND // 8) * 8


def _top_desc(cur, n):
    vals = []
    for r in range(n):
        m = jnp.max(cur, axis=0, keepdims=True)
        vals.append(m)
        if r + 1 < n:
            cur = jnp.where(cur >= m, -jnp.inf, cur)
    return vals


def _oddeven_merge(lo, hi, r):
    step = r * 2
    if step < hi - lo:
        yield from _oddeven_merge(lo, hi, step)
        yield from _oddeven_merge(lo + r, hi, step)
        yield from [(i, i + r) for i in range(lo + r, hi - r, step)]
    else:
        yield (lo, lo + r)


def _oddeven_sort(lo, hi):
    if hi - lo >= 1:
        mid = lo + (hi - lo) // 2
        yield from _oddeven_sort(lo, mid)
        yield from _oddeven_sort(mid + 1, hi)
        yield from _oddeven_merge(lo, hi, 1)


PEER_SLABS = PEER_NK // 8
PEER_SORT_NET = tuple(_oddeven_sort(0, PEER_SLABS - 1))


def _top_desc_keys(s, n):
    slabs = [s[8 * k:8 * k + 8, :] for k in range(PEER_SLABS)]
    for i, j in PEER_SORT_NET:
        hi = jnp.maximum(slabs[i], slabs[j])
        lo = jnp.minimum(slabs[i], slabs[j])
        slabs[i], slabs[j] = hi, lo
    vals = []
    for r in range(n):
        m = jnp.max(slabs[0], axis=0, keepdims=True)
        vals.append(m)
        if r + 1 < n:
            pop = slabs[0] >= m
            for k in range(min(PEER_SLABS, n - r - 1)):
                nxt = slabs[k + 1] if k + 1 < PEER_SLABS else -jnp.inf
                slabs[k] = jnp.where(pop, nxt, slabs[k])
    return vals


def _peer_prologue(st_ref, e0_ref, e1_ref, tau_ref, a1_ref, cand_ref):
    cand_ref[...] = jnp.full(cand_ref.shape, -jnp.inf, F32)
    for c in range(0, st_ref.shape[2], 128):
        lanes = slice(c, c + 128)
        for h in range(PEER_HEADS):
            s0 = st_ref[2 * h, :, lanes]
            s1 = st_ref[2 * h + 1, :, lanes]
            a0 = _top_desc_keys(s0, PEER_NTOP)
            a1 = _top_desc_keys(s1, PEER_NTOP)
            for b in range(PEER_NTOP):
                a1_ref[b:b + 1, lanes] = a1[b]
            off = 0
            for a, span in enumerate(PEER_CAND_SPANS):
                cand_ref[off:off + span, lanes] = a0[a] + a1_ref[0:span, lanes]
                off += span
            cand = cand_ref[:, lanes]
            top = _top_desc(cand, PEER_NTOP)
            tau = 0.5 * (top[PEER_TOPK - 1] + top[PEER_TOPK])
            z = jnp.sum(jnp.where(cand >= tau, jnp.exp(cand - top[0]), 0.0), axis=0, keepdims=True)
            e0_ref[h, :, lanes] = jnp.exp(s0 - a0[0])
            e1_ref[h, :, lanes] = jnp.exp(s1 - a1[0]) / z
            tau_ref[h, :, lanes] = tau


def _peer_kernel(st_ref, n_ref, h_ref, u_ref, vp_ref, vl_ref, gf_ref, o_ref,
                 e0_ref, e1_ref, tau_ref, a1_ref, cand_ref, hid_ref, a_ref, acc_ref, *, final_norm):
    j = pl.program_id(1)

    @pl.when(j == 0)
    def _():
        acc_ref[...] = jnp.zeros_like(acc_ref)
        a_ref[...] = jnp.zeros_like(a_ref)
        _peer_prologue(st_ref, e0_ref, e1_ref, tau_ref, a1_ref, cand_ref)

    cur = j % 2
    hid_ref[...] = _dot_nt(u_ref[...], n_ref[...])
    acc_ref[...] += _dot_tn(a_ref[1 - cur], vp_ref[...])
    for r in range(PEER_EBLK // PEER_NK):
        i0 = j * (PEER_EBLK // PEER_NK) + r
        rows = slice(r * PEER_NK, (r + 1) * PEER_NK)
        theta = [tau_ref[h] - st_ref[2 * h, pl.ds(i0, 1), :] for h in range(PEER_HEADS)]
        e0row = [e0_ref[h, pl.ds(i0, 1), :] for h in range(PEER_HEADS)]
        for c in range(0, st_ref.shape[2], 128):
            lanes = slice(c, c + 128)
            g = None
            for h in range(PEER_HEADS):
                w = (jnp.where(st_ref[2 * h + 1, :, lanes] >= theta[h][:, lanes], e1_ref[h, :, lanes], 0.0)
                     * e0row[h][:, lanes])
                g = w if g is None else g + w
            a_ref[cur, rows, lanes] = (g * _gelu(hid_ref[rows, lanes])).astype(BF16)

    @pl.when(j == pl.num_programs(1) - 1)
    def _():
        out = h_ref[...] + acc_ref[...] + _dot_tn(a_ref[cur], vl_ref[...])
        if final_norm:
            out = _rms(out, gf_ref[...])
        o_ref[...] = out


def _peer(st, n2, h, tables, gfin, tm, final_norm):
    u, v, layer = tables
    t = n2.shape[0]
    nj = PEER_EXPERTS // PEER_EBLK
    return pl.pallas_call(
        functools.partial(_peer_kernel, final_norm=final_norm),
        out_shape=jax.ShapeDtypeStruct((t, D_MODEL), F32),
        grid=(t // tm, nj),
        in_specs=[pl.BlockSpec((2 * PEER_HEADS, PEER_NK, tm), lambda i, j: (0, 0, i)),
                  pl.BlockSpec((tm, D_MODEL), lambda i, j: (i, 0)),
                  pl.BlockSpec((tm, D_MODEL), lambda i, j: (i, 0)),
                  pl.BlockSpec((None, PEER_EBLK, D_MODEL), lambda i, j: (layer, j, 0)),
                  pl.BlockSpec((None, PEER_EBLK, D_MODEL), lambda i, j: (layer, jnp.maximum(j - 1, 0), 0)),
                  pl.BlockSpec((None, PEER_EBLK, D_MODEL), lambda i, j: (layer, nj - 1, 0)),
                  pl.BlockSpec((1, D_MODEL), lambda i, j: (0, 0))],
        out_specs=pl.BlockSpec((tm, D_MODEL), lambda i, j: (i, 0)),
        scratch_shapes=[pltpu.VMEM((PEER_HEADS, PEER_NK, tm), F32), pltpu.VMEM((PEER_HEADS, PEER_NK, tm), F32),
                        pltpu.VMEM((PEER_HEADS, 1, tm), F32),
                        pltpu.VMEM((PEER_NTOP + 7, tm), F32), pltpu.VMEM((PEER_CAND_ROWS, tm), F32),
                        pltpu.VMEM((PEER_EBLK, tm), F32), pltpu.VMEM((2, PEER_EBLK, tm), BF16),
                        pltpu.VMEM((tm, D_MODEL), F32)],
        compiler_params=_cparams(("parallel", "arbitrary")),
        name="peer",
    )(st, n2, h, u, v, v, gfin)


def _blockdiag(x, n):
    nm, r, c = x.shape
    m = nm // n
    return jnp.einsum("mgrc,gh->mgrhc", x.reshape(m, n, r, c), jnp.eye(n, dtype=x.dtype)).reshape(m, n * r, n * c)


def _s5_weights(lam_re, lam_im, log_dt, b_re, b_im, c_re, c_im):
    dt = jnp.exp(log_dt)[:, None]
    mag = jnp.exp(lam_re * dt)
    ab_re = mag * jnp.cos(lam_im * dt)
    ab_im = mag * jnp.sin(lam_im * dt)
    den = lam_re * lam_re + lam_im * lam_im
    zr = ((ab_re - 1.0) * lam_re + ab_im * lam_im) / den
    zi = (ab_im * lam_re - (ab_re - 1.0) * lam_im) / den
    bb_re = zr[..., None] * b_re - zi[..., None] * b_im
    bb_im = zr[..., None] * b_im + zi[..., None] * b_re
    eye = jnp.eye(S5_GROUPS, dtype=F32)
    gpb = S5_GROUPS // S5_LANE_BLOCKS
    b_blocks = lambda b: _blockdiag(b.transpose(0, 2, 1), gpb).astype(BF16)
    c_blocks = lambda c: _blockdiag(c.transpose(0, 2, 1), gpb).astype(BF16)
    return {
        "ab_re": ab_re.reshape(1, -1), "ab_im": ab_im.reshape(1, -1),
        "bc_re": b_blocks(bb_re), "bc_im": b_blocks(bb_im), "cc_re": c_blocks(c_re), "cc_im": c_blocks(c_im),
        "bd_re": jnp.einsum("gpc,gh->gchp", bb_re, eye).reshape(MIX, -1).astype(BF16),
        "bd_im": jnp.einsum("gpc,gh->gchp", bb_im, eye).reshape(MIX, -1).astype(BF16),
        "cd_re": jnp.einsum("gcp,gh->gphc", c_re, eye).reshape(-1, MIX).astype(BF16),
        "cd_im": jnp.einsum("gcp,gh->gphc", c_im, eye).reshape(-1, MIX).astype(BF16),
    }


def _ret_consts():
    ch = RET_CHUNK
    log_g = jnp.log1p(-jnp.exp2(-5.0 - jnp.arange(HEADS, dtype=F32)))
    idx = jnp.arange(ch, dtype=F32)
    diff = idx[:, None] - idx[None, :]
    mask = diff >= 0
    dec = jnp.where(mask[None], jnp.exp(log_g[:, None, None] * jnp.where(mask, diff, 0.0)[None]), 0.0)
    lane_head = (jnp.arange(QK) % 128) // 32
    lg_lane = log_g[lane_head]
    wq = jnp.exp(lg_lane[None, :] * (idx + 1.0)[:, None])
    wk = jnp.exp(lg_lane[None, :] * (ch - 1.0 - idx)[:, None])
    gc = jnp.repeat(jnp.exp(log_g * ch), DV)[None, :]
    hmask = (lane_head[None, :] == jnp.arange(HEADS)[:, None]).astype(F32)[:, None, :]
    rdec = jnp.broadcast_to(jnp.exp(log_g)[:, None, None], (HEADS, 1, DV))
    return {"dec": dec, "wq": wq, "wk": wk, "gc": gc, "hmask": hmask, "rdec": rdec}


def _gla_consts():
    tril = jnp.tril(jnp.ones((GLA_CHUNK, GLA_CHUNK), F32))
    hmask = ((jnp.arange(QK) // DK)[None, :] == jnp.arange(HEADS)[:, None]).astype(F32)[:, None, :]
    return {"tril": tril, "hmask": hmask}


def _rope_tables(pos):
    half = DK // 2
    inv = ROPE_BASE ** (-jnp.arange(half, dtype=F32) / half)
    ang = pos.astype(F32)[:, None] * inv[None, :]
    return jnp.tile(jnp.cos(ang), (1, HEADS)), jnp.tile(jnp.sin(ang), (1, HEADS))


def _layer_weights(lp):
    w_in = lp["w_in"]
    cuts = {}
    off = 0
    for name, size in (("u", 512), ("rq", 256), ("rk", 256), ("rv", 512), ("rg", 512), ("lx", 512), ("lg", 512),
                       ("gq", 256), ("gk", 256), ("gv", 512), ("ga", GLA_RANK), ("gg", 512), ("mg", 4096)):
        cuts[name] = w_in[:, off:off + size]
        off += size

    def split_halves(w):
        return w.reshape(D_MODEL, HEADS, 2, 32).transpose(0, 2, 1, 3).reshape(D_MODEL, QK)

    w_main = jnp.concatenate([cuts["mg"], cuts["u"], split_halves(cuts["rq"]), split_halves(cuts["rk"]), cuts["rv"],
                              cuts["rg"], cuts["lx"], cuts["lg"], cuts["gq"], cuts["gk"], cuts["gv"], cuts["gg"]],
                             axis=1).astype(BF16)
    w_ga = jnp.pad(cuts["ga"], ((0, 0), (0, 128 - GLA_RANK))).astype(BF16)
    row = lambda v: v.reshape(1, -1)
    return {
        "norm_mix": row(lp["norm_mix"]), "w_main": w_main, "w_ga": w_ga,
        "wa2": jnp.pad(lp["gla_w_a2"], ((0, 128 - GLA_RANK), (0, 0))).astype(BF16), "ba2": row(lp["gla_b_a2"]),
        "s5": _s5_weights(lp["s5_lambda_re"], lp["s5_lambda_im"], lp["s5_log_dt"], lp["s5_b_re"], lp["s5_b_im"],
                          lp["s5_c_re"], lp["s5_c_im"]),
        "lru": {"conv_w": lp["lru_conv_w"], "conv_b": row(lp["lru_conv_b"]),
                "wa": _blockdiag(lp["lru_w_a"], LRU_BLOCKS)[0].astype(BF16), "ba": row(lp["lru_b_a"]),
                "wx": _blockdiag(lp["lru_w_x"], LRU_BLOCKS)[0].astype(BF16), "bx": row(lp["lru_b_x"]),
                "lam": row(lp["lru_lambda"])},
        "merge": {"s5_d": row(lp["s5_d"]), "w_glu": lp["s5_w_glu"].astype(BF16), "b_glu": row(lp["s5_b_glu"]),
                  "ret_norm": row(lp["ret_norm"]), "gla_norm": row(lp["gla_norm"]),
                  "w_branch": lp["w_branch"].astype(BF16), "w_out": lp["w_out"].astype(BF16),
                  "norm_ffn": row(lp["norm_ffn"])},
        "peer_wq": lp["peer_w_q"].astype(BF16),
        "peer_sk": lp["peer_sub_keys"].reshape(2 * PEER_HEADS, PEER_NK, PEER_NK).astype(BF16),
    }


def _token_tail(x, p, ys, ro, yc, go, w, gfin, final_norm, tm):
    h, n2 = _merge(x, p, ys, ro, yc, go, w["merge"], tm)
    st = _scores(n2, w["peer_wq"], w["peer_sk"], tm)
    return _peer(st, n2, h, w["peer_tables"], gfin, tm, final_norm)


def _layer_prompt(x, w, rc, gc, rope, gfin, final_norm, bsz, seq, tm):
    t = bsz * seq
    p, ga = _inproj(x, w["norm_mix"], w["w_main"], w["w_ga"], math.gcd(t, INPROJ_ROWS))
    qr, kr, la = _prep(p, ga, rope[0], rope[1], w["wa2"], w["ba2"], tm)
    p3 = p.reshape(bsz, seq, IN_MAIN)
    ys, s5_re, s5_im = _s5_prompt(p3, w["s5"], bsz, seq)
    ro, s_ret = _ret_prompt(qr, kr, p, rc, bsz, seq)
    go, s_gla = _gla_prompt(p, la, gc, bsz, seq)
    yc, s_lru = _lru_prompt(p3, w["lru"], bsz, seq)
    s_conv = p3[:, seq - (CONV_W - 1):, COL_LX:COL_LX + MIX]
    out = _token_tail(x, p, ys, ro, yc, go, w, gfin, final_norm, tm)
    return out, (s5_re, s5_im, s_ret, s_lru, s_conv, s_gla)


def _layer_sample(x, states, layer, w, rc, rope, gfin, final_norm):
    t = x.shape[0]
    s5_re0, s5_im0, _, lru0, conv0, _ = (s[layer] for s in states)
    p, ga = _inproj(x, w["norm_mix"], w["w_main"], w["w_ga"], t)
    qr, kr, la = _prep(p, ga, rope[0], rope[1], w["wa2"], w["ba2"], t)
    ys, s5_re, s5_im = _s5_step(p, s5_re0.reshape(t, -1), s5_im0.reshape(t, -1), w["s5"])
    ro, s_ret, go, s_gla = _mat_step(qr, kr, la, p, states[2], states[5], layer, rc["rdec"])
    c0 = conv0.transpose(1, 0, 2).reshape((CONV_W - 1) * t, MIX)
    yc, s_lru, cf = _lru(p[:, COL_LX:COL_LX + MIX], p[:, COL_LG:COL_LG + MIX], lru0, c0, w["lru"], t, 1)
    s_conv = cf.reshape(CONV_W - 1, t, MIX).transpose(1, 0, 2)
    out = _token_tail(x, p, ys, ro, yc, go, w, gfin, final_norm, t)
    return out, (s5_re.reshape(s5_re0.shape), s5_im.reshape(s5_im0.shape), s_ret, s_lru, s_conv, s_gla)


_PARAM_NAMES = ("norm_mix", "w_in", "s5_lambda_re", "s5_lambda_im", "s5_log_dt", "s5_b_re", "s5_b_im", "s5_c_re",
                "s5_c_im", "s5_d", "s5_w_glu", "s5_b_glu", "ret_norm", "lru_conv_w", "lru_conv_b", "lru_w_a",
                "lru_b_a", "lru_w_x", "lru_b_x", "lru_lambda", "gla_w_a2", "gla_b_a2", "gla_norm", "w_branch",
                "w_out", "norm_ffn", "peer_w_q", "peer_sub_keys", "peer_u", "peer_v")


def _forward(x_prompt, x_sample, states, params, norm_final, tm):
    bsz, seq, _ = x_prompt.shape
    nsmp = x_sample.shape[0]
    depth = params["w_in"].shape[0]
    rc = _ret_consts()
    gc = _gla_consts()
    rope_p = _rope_tables(jnp.arange(seq, dtype=jnp.int32))
    rope_s = _rope_tables(jnp.full((nsmp,), PAST_LEN, dtype=jnp.int32))
    gfin = norm_final.reshape(1, -1)
    hp = x_prompt.reshape(bsz * seq, D_MODEL)
    hs = x_sample.reshape(nsmp, D_MODEL)
    new_p, new_s = [], []
    peer_u = params["peer_u"].astype(BF16)
    peer_v = params["peer_v"].astype(BF16)
    for l in range(depth):
        w = _layer_weights({k: v[l] for k, v in params.items() if k not in ("peer_u", "peer_v")})
        w["peer_tables"] = (peer_u, peer_v, l)
        last = l == depth - 1
        hp, sp = _layer_prompt(hp, w, rc, gc, rope_p, gfin, last, bsz, seq, tm)
        hs, ss = _layer_sample(hs, states, l, w, rc, rope_s, gfin, last)
        new_p.append(sp)
        new_s.append(ss)
    outs = [hp.reshape(bsz, seq, D_MODEL), hs.reshape(nsmp, 1, D_MODEL)]
    outs += [jnp.stack([s[i] for s in new_p]) for i in range(6)]
    outs += [jnp.stack([s[i] for s in new_s]) for i in range(6)]
    return tuple(outs)


def kernel(x_prompt, x_sample, state_s5_re, state_s5_im, state_ret, state_lru, state_conv, state_gla, norm_mix, w_in, s5_lambda_re, s5_lambda_im, s5_log_dt, s5_b_re, s5_b_im, s5_c_re, s5_c_im, s5_d, s5_w_glu, s5_b_glu, ret_norm, lru_conv_w, lru_conv_b, lru_w_a, lru_b_a, lru_w_x, lru_b_x, lru_lambda, gla_w_a2, gla_b_a2, gla_norm, w_branch, w_out, norm_ffn, peer_w_q, peer_sub_keys, peer_u, peer_v, norm_final):
    values = (norm_mix, w_in, s5_lambda_re, s5_lambda_im, s5_log_dt, s5_b_re, s5_b_im, s5_c_re, s5_c_im, s5_d,
              s5_w_glu, s5_b_glu, ret_norm, lru_conv_w, lru_conv_b, lru_w_a, lru_b_a, lru_w_x, lru_b_x, lru_lambda,
              gla_w_a2, gla_b_a2, gla_norm, w_branch, w_out, norm_ffn, peer_w_q, peer_sub_keys, peer_u, peer_v)
    params = dict(zip(_PARAM_NAMES, values))
    states = (state_s5_re, state_s5_im, state_ret, state_lru, state_conv, state_gla)
    return _forward(x_prompt, x_sample, states, params, norm_final, 256)
```
